```python
import jax, jax.numpy as jnp
from jax import lax
import numpy as np

D_MODEL = 2048
BATCH = 2
SEQ = 4096
DEPTH = 4

D_A = D_MODEL // 2
D_B = D_MODEL // 2
HEAD_DIM = 128
N_HEADS_A = D_A // HEAD_DIM
N_HEADS_B = D_B // HEAD_DIM
D_IN_EVEN = 2 * D_A + 3 * D_B
CONV_A_WIDTH = 31
CONV_B_WIDTH = 3
POOL_WINDOWS = (2, 4, 8, 16)
N_POOL_GROUPS = len(POOL_WINDOWS)
POOL_GROUP_DIM = D_MODEL // N_POOL_GROUPS
D_FF = 4 * D_MODEL
N_EVEN = (DEPTH + 1) // 2
N_ODD = DEPTH // 2
RMS_EPS = 1e-6
LN_EPS = 1e-5

kernel_name = "hybrid_conformerconv_shortconv_multipool_trunk"


def rmsnorm(x, g):
    xf = x.astype(jnp.float32)
    y = xf * lax.rsqrt(jnp.mean(xf * xf, axis=-1, keepdims=True) + RMS_EPS)
    return (y * g.astype(jnp.float32)).astype(x.dtype)


def layernorm(x, g, b):
    xf = x.astype(jnp.float32)
    mu = jnp.mean(xf, axis=-1, keepdims=True)
    xc = xf - mu
    var = jnp.mean(xc * xc, axis=-1, keepdims=True)
    y = xc * lax.rsqrt(var + LN_EPS) * g.astype(jnp.float32) + b.astype(jnp.float32)
    return y.astype(x.dtype)


def causal_depthwise_conv(x, w):
    k_width, channels = w.shape
    return lax.conv_general_dilated(
        x, w[:, None, :].astype(x.dtype),
        window_strides=(1,),
        padding=[(k_width - 1, 0)],
        dimension_numbers=("NWC", "WIO", "NWC"),
        feature_group_count=channels)


def conv_mixers(h, w_in, conv_a_w, conv_a_b, ln_a_g, ln_a_b, conv_b_w, w_out):
    u = jnp.einsum("bsd,de->bse", h, w_in)
    a_val, a_gate, b_x, b_c, b_b = jnp.split(
        u, [D_A, 2 * D_A, 2 * D_A + D_B, 2 * D_A + 2 * D_B], axis=-1)
    a = a_val * jax.nn.sigmoid(a_gate)
    a = causal_depthwise_conv(a, conv_a_w) + conv_a_b
    a = jax.nn.silu(layernorm(a, ln_a_g, ln_a_b))
    bo = b_b * causal_depthwise_conv(b_c * b_x, conv_b_w)
    return jnp.einsum("bse,ed->bsd", jnp.concatenate([a, bo], axis=-1), w_out)


def pool_mixer(h, pool_w, pool_scale):
    bsz, seq, _ = h.shape
    hg = h.reshape(bsz, seq, N_POOL_GROUPS, POOL_GROUP_DIM)
    csum = jnp.cumsum(hg.astype(jnp.float32), axis=1)
    pos = jnp.arange(1, seq + 1, dtype=jnp.float32)
    means = []
    for g, win in enumerate(POOL_WINDOWS):
        cg = csum[:, :, g]
        lagged = jnp.pad(cg, ((0, 0), (win, 0), (0, 0)))[:, :seq]
        count = jnp.minimum(pos, float(win))[None, :, None]
        means.append((cg - lagged) / count)
    pooled = jnp.stack(means, axis=2).astype(h.dtype) - hg
    mixed = jnp.einsum("bsgc,gce->bsge", pooled, pool_w).reshape(bsz, seq, D_MODEL)
    return mixed * pool_scale


def sq_relu_mlp(h, w1, w2):
    z = jnp.einsum("bsd,df->bsf", h, w1)
    z = jnp.square(jax.nn.relu(z))
    return jnp.einsum("bsf,fd->bsd", z, w2)


def setup_inputs(seed: int = 0) -> dict:
    key = jax.random.key(seed)
    ks = jax.random.split(key, 16)
    f32 = jnp.float32
    nrm = lambda k, shape, scale: jax.random.normal(k, shape, f32) * scale
    return {
        "x": jax.random.normal(ks[0], (BATCH, SEQ, D_MODEL), f32),
        "norm_mix_g": 1.0 + nrm(ks[1], (DEPTH, D_MODEL), 0.05),
        "norm_mlp_g": 1.0 + nrm(ks[2], (DEPTH, D_MODEL), 0.05),
        "w_in_even": nrm(ks[3], (N_EVEN, D_MODEL, D_IN_EVEN), D_MODEL ** -0.5),
        "conv_a_w": nrm(ks[4], (N_EVEN, CONV_A_WIDTH, D_A), CONV_A_WIDTH ** -0.5),
        "conv_a_b": nrm(ks[5], (N_EVEN, D_A), 0.02),
        "ln_a_g": 1.0 + nrm(ks[6], (N_EVEN, D_A), 0.05),
        "ln_a_b": nrm(ks[7], (N_EVEN, D_A), 0.02),
        "conv_b_w": nrm(ks[8], (N_EVEN, CONV_B_WIDTH, D_B), CONV_B_WIDTH ** -0.5),
        "w_out_even": nrm(ks[9], (N_EVEN, D_A + D_B, D_MODEL), (D_A + D_B) ** -0.5),
        "pool_w": nrm(ks[10], (N_ODD, N_POOL_GROUPS, POOL_GROUP_DIM, POOL_GROUP_DIM), POOL_GROUP_DIM ** -0.5),
        "pool_scale": 1.0 + nrm(ks[11], (N_ODD, D_MODEL), 0.1),
        "mlp_w1": nrm(ks[12], (DEPTH, D_MODEL, D_FF), D_MODEL ** -0.5),
        "mlp_w2": nrm(ks[13], (DEPTH, D_FF, D_MODEL), 0.5 * D_FF ** -0.5),
        "final_g": 1.0 + nrm(ks[14], (D_MODEL,), 0.05),
    }


def reference(x, norm_mix_g, norm_mlp_g, w_in_even, conv_a_w, conv_a_b, ln_a_g, ln_a_b,
              conv_b_w, w_out_even, pool_w, pool_scale, mlp_w1, mlp_w2, final_g):
    for layer in range(DEPTH):
        i = layer // 2
        h = rmsnorm(x, norm_mix_g[layer])
        if layer % 2 == 0:
            x = x + conv_mixers(h, w_in_even[i], conv_a_w[i], conv_a_b[i], ln_a_g[i],
                                ln_a_b[i], conv_b_w[i], w_out_even[i])
        else:
            x = x + pool_mixer(h, pool_w[i], pool_scale[i])
        h = rmsnorm(x, norm_mlp_g[layer])
        x = x + sq_relu_mlp(h, mlp_w1[layer], mlp_w2[layer])
    return rmsnorm(x, final_g)
```

```python
import functools

import jax
import jax.numpy as jnp
from jax import lax
from jax.experimental import pallas as pl
from jax.experimental.pallas import tpu as pltpu

D_MODEL = 2048
D_A = D_MODEL // 2
D_B = D_MODEL // 2
D_IN_EVEN = 2 * D_A + 3 * D_B
CONV_A_WIDTH = 31
CONV_B_WIDTH = 3
POOL_WINDOWS = (2, 4, 8, 16)
POOL_GROUP_DIM = D_MODEL // len(POOL_WINDOWS)
D_FF = 4 * D_MODEL
RMS_EPS = 1e-6
LN_EPS = 1e-5

VMEM_LIMIT_BYTES = 56 * 1024 * 1024

A_HIST = 32
B_HIST = 8
POOL_HIST = 16

TM_EVEN = 256
TM_POOL = 512
TM_MLP = 1024
TF_MLP = 512


def _rms_scale(x):
    ms = jnp.mean(x * x, axis=-1, keepdims=True)
    return x * lax.rsqrt(ms + RMS_EPS)


def _const_spec(shape):
    zeros = (0,) * len(shape)
    return pl.BlockSpec(shape, lambda *_: zeros, pipeline_mode=pl.Buffered(1))


def _even_mixer_kernel(x_ref, g_ref, win_ref, caw_ref, cab_ref, lng_ref, lnb_ref,
                       cbw_ref, wout_ref, o_ref, abuf, bbuf):
    tm = x_ref.shape[1]

    @pl.when(pl.program_id(1) == 0)
    def _():
        abuf[0:A_HIST, :] = jnp.zeros((A_HIST, D_A), jnp.float32)
        bbuf[0:B_HIST, :] = jnp.zeros((B_HIST, D_B), jnp.float32)

    x = x_ref[0]
    h = (_rms_scale(x) * g_ref[...]).astype(jnp.bfloat16)

    def proj(lo, width):
        return jnp.dot(h, win_ref[:, lo:lo + width], preferred_element_type=jnp.float32)

    a_val = proj(0, D_A)
    a_gate = proj(D_A, D_A)
    abuf[A_HIST:A_HIST + tm, :] = a_val * jax.nn.sigmoid(a_gate)
    acc = jnp.broadcast_to(cab_ref[...], (tm, D_A))
    for k in range(CONV_A_WIDTH):
        off = A_HIST - (CONV_A_WIDTH - 1) + k
        acc = acc + caw_ref[k:k + 1, :] * abuf[off:off + tm, :]
    mu = jnp.mean(acc, axis=-1, keepdims=True)
    xc = acc - mu
    var = jnp.mean(xc * xc, axis=-1, keepdims=True)
    y = xc * lax.rsqrt(var + LN_EPS) * lng_ref[...] + lnb_ref[...]
    a_out = y * jax.nn.sigmoid(y)

    b_x = proj(2 * D_A, D_B)
    b_c = proj(2 * D_A + D_B, D_B)
    bbuf[B_HIST:B_HIST + tm, :] = b_c * b_x
    accb = jnp.zeros((tm, D_B), jnp.float32)
    for k in range(CONV_B_WIDTH):
        off = B_HIST - (CONV_B_WIDTH - 1) + k
        accb = accb + cbw_ref[k:k + 1, :] * bbuf[off:off + tm, :]
    b_b = proj(2 * D_A + 2 * D_B, D_B)
    bo = b_b * accb

    cat = jnp.concatenate([a_out, bo], axis=-1).astype(jnp.bfloat16)
    o_ref[0] = x + jnp.dot(cat, wout_ref[...], preferred_element_type=jnp.float32)

    abuf[0:A_HIST, :] = abuf[tm:tm + A_HIST, :]
    bbuf[0:B_HIST, :] = bbuf[tm:tm + B_HIST, :]


def _even_mixer(x, g, w_in, conv_a_w, conv_a_b, ln_g, ln_b, conv_b_w, w_out):
    bsz, seq, d = x.shape
    tm = TM_EVEN
    caw = jnp.pad(conv_a_w, ((0, 32 - CONV_A_WIDTH), (0, 0)))
    cbw = jnp.pad(conv_b_w, ((0, 8 - CONV_B_WIDTH), (0, 0)))
    row = lambda v: v.reshape(1, -1)
    return pl.pallas_call(
        _even_mixer_kernel,
        name="even_mixer",
        grid=(bsz, seq // tm),
        in_specs=[
            pl.BlockSpec((1, tm, d), lambda b, j: (b, j, 0)),
            _const_spec((1, d)),
            _const_spec((d, D_IN_EVEN)),
            _const_spec((32, D_A)),
            _const_spec((1, D_A)),
            _const_spec((1, D_A)),
            _const_spec((1, D_A)),
            _const_spec((8, D_B)),
            _const_spec((D_A + D_B, d)),
        ],
        out_specs=pl.BlockSpec((1, tm, d), lambda b, j: (b, j, 0)),
        out_shape=jax.ShapeDtypeStruct(x.shape, x.dtype),
        scratch_shapes=[
            pltpu.VMEM((A_HIST + tm, D_A), jnp.float32),
            pltpu.VMEM((B_HIST + tm, D_B), jnp.float32),
        ],
        compiler_params=pltpu.CompilerParams(
            dimension_semantics=("arbitrary", "arbitrary"),
            vmem_limit_bytes=VMEM_LIMIT_BYTES),
    )(x, row(g), w_in, caw, row(conv_a_b), row(ln_g), row(ln_b), cbw, w_out)


def _pool_mixer_kernel(x_ref, g_ref, pw_ref, ps_ref, o_ref, hbuf):
    tm = x_ref.shape[1]
    j = pl.program_id(1)

    @pl.when(j == 0)
    def _():
        hbuf[0:POOL_HIST, :] = jnp.zeros((POOL_HIST, D_MODEL), jnp.float32)

    x = x_ref[0]
    hbuf[POOL_HIST:POOL_HIST + tm, :] = _rms_scale(x) * g_ref[...]
    pos = (j * tm + 1 + lax.broadcasted_iota(jnp.int32, (tm, 1), 0)).astype(jnp.float32)

    for grp, win in enumerate(POOL_WINDOWS):
        lo = grp * POOL_GROUP_DIM
        cols = slice(lo, lo + POOL_GROUP_DIM)
        hg = hbuf[POOL_HIST:POOL_HIST + tm, cols]
        s = hg
        for lag in range(1, win):
            s = s + hbuf[POOL_HIST - lag:POOL_HIST - lag + tm, cols]
        pooled = s / jnp.minimum(pos, float(win)) - hg
        mixed = jnp.dot(pooled.astype(jnp.bfloat16), pw_ref[grp],
                        preferred_element_type=jnp.float32)
        o_ref[0, :, cols] = x[:, cols] + mixed * ps_ref[:, cols]

    hbuf[0:POOL_HIST, :] = hbuf[tm:tm + POOL_HIST, :]


def _pool_mixer(x, g, pool_w, pool_scale):
    bsz, seq, d = x.shape
    tm = TM_POOL
    return pl.pallas_call(
        _pool_mixer_kernel,
        name="pool_mixer",
        grid=(bsz, seq // tm),
        in_specs=[
            pl.BlockSpec((1, tm, d), lambda b, j: (b, j, 0)),
            _const_spec((1, d)),
            _const_spec(pool_w.shape),
            _const_spec((1, d)),
        ],
        out_specs=pl.BlockSpec((1, tm, d), lambda b, j: (b, j, 0)),
        out_shape=jax.ShapeDtypeStruct(x.shape, x.dtype),
        scratch_shapes=[pltpu.VMEM((POOL_HIST + tm, d), jnp.float32)],
        compiler_params=pltpu.CompilerParams(
            dimension_semantics=("arbitrary", "arbitrary"),
            vmem_limit_bytes=VMEM_LIMIT_BYTES),
    )(x, g.reshape(1, -1), pool_w, pool_scale.reshape(1, -1))


def _mlp_kernel(x_ref, g_ref, w1_ref, w2_ref, fg_ref, o_ref, h_ref, *, apply_final_norm):
    f = pl.program_id(1)

    @pl.when(f == 0)
    def _():
        x = x_ref[...]
        h_ref[...] = (_rms_scale(x) * g_ref[...]).astype(jnp.bfloat16)
        o_ref[...] = x

    z = jnp.dot(h_ref[...], w1_ref[...], preferred_element_type=jnp.float32)
    z = jnp.square(jnp.maximum(z, 0.0)).astype(jnp.bfloat16)
    o_ref[...] += jnp.dot(z, w2_ref[...], preferred_element_type=jnp.float32)

    if apply_final_norm:
        @pl.when(f == pl.num_programs(1) - 1)
        def _():
            o_ref[...] = _rms_scale(o_ref[...]) * fg_ref[...]


def _mlp(x2d, g, w1, w2, final_g, apply_final_norm):
    n, d = x2d.shape
    tm, tf = TM_MLP, TF_MLP
    return pl.pallas_call(
        functools.partial(_mlp_kernel, apply_final_norm=apply_final_norm),
        name="mlp",
        grid=(n // tm, D_FF // tf),
        in_specs=[
            pl.BlockSpec((tm, d), lambda i, f: (i, 0)),
            _const_spec((1, d)),
            pl.BlockSpec((d, tf), lambda i, f: (0, f)),
            pl.BlockSpec((tf, d), lambda i, f: (f, 0)),
            _const_spec((1, d)),
        ],
        out_specs=pl.BlockSpec((tm, d), lambda i, f: (i, 0)),
        out_shape=jax.ShapeDtypeStruct(x2d.shape, x2d.dtype),
        scratch_shapes=[pltpu.VMEM((tm, d), jnp.bfloat16)],
        compiler_params=pltpu.CompilerParams(
            dimension_semantics=("arbitrary", "arbitrary"),
            vmem_limit_bytes=VMEM_LIMIT_BYTES),
    )(x2d, g.reshape(1, -1), w1, w2, final_g.reshape(1, -1))


def kernel(x, norm_mix_g, norm_mlp_g, w_in_even, conv_a_w, conv_a_b, ln_a_g, ln_a_b,
           conv_b_w, w_out_even, pool_w, pool_scale, mlp_w1, mlp_w2, final_g):
    bsz, seq, d = x.shape
    depth = norm_mix_g.shape[0]
    bf16 = jnp.bfloat16
    for layer in range(depth):
        i = layer // 2
        if layer % 2 == 0:
            x = _even_mixer(x, norm_mix_g[layer], w_in_even[i].astype(bf16), conv_a_w[i],
                            conv_a_b[i], ln_a_g[i], ln_a_b[i], conv_b_w[i],
                            w_out_even[i].astype(bf16))
        else:
            x = _pool_mixer(x, norm_mix_g[layer], pool_w[i].astype(bf16), pool_scale[i])
        x = _mlp(x.reshape(bsz * seq, d), norm_mlp_g[layer], mlp_w1[layer].astype(bf16),
                 mlp_w2[layer].astype(bf16), final_g,
                 apply_final_norm=(layer == depth - 1)).reshape(bsz, seq, d)
    return x
```

```python
import functools

import jax
import jax.numpy as jnp
from jax import lax
from jax.experimental import pallas as pl
from jax.experimental.pallas import tpu as pltpu

D_MODEL = 2048
D_A = D_MODEL // 2
D_B = D_MODEL // 2
D_IN_EVEN = 2 * D_A + 3 * D_B
CONV_A_WIDTH = 31
CONV_B_WIDTH = 3
POOL_WINDOWS = (2, 4, 8, 16)
POOL_GROUP_DIM = D_MODEL // len(POOL_WINDOWS)
D_FF = 4 * D_MODEL
RMS_EPS = 1e-6
LN_EPS = 1e-5

VMEM_LIMIT_BYTES = 56 * 1024 * 1024

A_HIST = 32
B_HIST = 8
POOL_HIST = 16

TM_EVEN = 256
TM_POOL = 512
TM_MLP = 1024
TF_MLP = 512


def _rms_scale(x):
    ms = jnp.mean(x * x, axis=-1, keepdims=True)
    return x * lax.rsqrt(ms + RMS_EPS)


def _const_spec(shape):
    zeros = (0,) * len(shape)
    return pl.BlockSpec(shape, lambda *_: zeros, pipeline_mode=pl.Buffered(1))


def _cast_kernel(w_ref, o_ref):
    o_ref[...] = w_ref[...].astype(o_ref.dtype)


def _layer_weight_bf16(w_all, layer, block_rows):
    _, rows, cols = w_all.shape
    assert rows % block_rows == 0
    return pl.pallas_call(
        _cast_kernel,
        name="cast_bf16",
        grid=(rows // block_rows,),
        in_specs=[pl.BlockSpec((None, block_rows, cols), lambda r: (layer, r, 0))],
        out_specs=pl.BlockSpec((block_rows, cols), lambda r: (r, 0)),
        out_shape=jax.ShapeDtypeStruct((rows, cols), jnp.bfloat16),
        compiler_params=pltpu.CompilerParams(
            dimension_semantics=("arbitrary",),
            vmem_limit_bytes=VMEM_LIMIT_BYTES),
    )(w_all)


def _even_mixer_kernel(x_ref, g_ref, win_ref, caw_ref, cab_ref, lng_ref, lnb_ref,
                       cbw_ref, wout_ref, o_ref, abuf, bbuf):
    tm = x_ref.shape[1]

    @pl.when(pl.program_id(1) == 0)
    def _():
        abuf[0:A_HIST, :] = jnp.zeros((A_HIST, D_A), jnp.float32)
        bbuf[0:B_HIST, :] = jnp.zeros((B_HIST, D_B), jnp.float32)

    x = x_ref[0]
    h = (_rms_scale(x) * g_ref[...]).astype(jnp.bfloat16)

    def proj(lo, width):
        return jnp.dot(h, win_ref[:, lo:lo + width], preferred_element_type=jnp.float32)

    a_val = proj(0, D_A)
    a_gate = proj(D_A, D_A)
    abuf[A_HIST:A_HIST + tm, :] = a_val * jax.nn.sigmoid(a_gate)
    acc = jnp.broadcast_to(cab_ref[...], (tm, D_A))
    for k in range(CONV_A_WIDTH):
        off = A_HIST - (CONV_A_WIDTH - 1) + k
        acc = acc + caw_ref[k:k + 1, :] * abuf[off:off + tm, :]
    mu = jnp.mean(acc, axis=-1, keepdims=True)
    xc = acc - mu
    var = jnp.mean(xc * xc, axis=-1, keepdims=True)
    y = xc * lax.rsqrt(var + LN_EPS) * lng_ref[...] + lnb_ref[...]
    a_out = y * jax.nn.sigmoid(y)

    b_x = proj(2 * D_A, D_B)
    b_c = proj(2 * D_A + D_B, D_B)
    bbuf[B_HIST:B_HIST + tm, :] = b_c * b_x
    accb = jnp.zeros((tm, D_B), jnp.float32)
    for k in range(CONV_B_WIDTH):
        off = B_HIST - (CONV_B_WIDTH - 1) + k
        accb = accb + cbw_ref[k:k + 1, :] * bbuf[off:off + tm, :]
    b_b = proj(2 * D_A + 2 * D_B, D_B)
    bo = b_b * accb

    cat = jnp.concatenate([a_out, bo], axis=-1).astype(jnp.bfloat16)
    o_ref[0] = x + jnp.dot(cat, wout_ref[...], preferred_element_type=jnp.float32)

    abuf[0:A_HIST, :] = abuf[tm:tm + A_HIST, :]
    bbuf[0:B_HIST, :] = bbuf[tm:tm + B_HIST, :]


def _even_mixer(x, g, w_in, conv_a_w, conv_a_b, ln_g, ln_b, conv_b_w, w_out):
    bsz, seq, d = x.shape
    tm = TM_EVEN
    caw = jnp.pad(conv_a_w, ((0, 32 - CONV_A_WIDTH), (0, 0)))
    cbw = jnp.pad(conv_b_w, ((0, 8 - CONV_B_WIDTH), (0, 0)))
    row = lambda v: v.reshape(1, -1)
    return pl.pallas_call(
        _even_mixer_kernel,
        name="even_mixer",
        grid=(bsz, seq // tm),
        in_specs=[
            pl.BlockSpec((1, tm, d), lambda b, j: (b, j, 0)),
            _const_spec((1, d)),
            _const_spec((d, D_IN_EVEN)),
            _const_spec((32, D_A)),
            _const_spec((1, D_A)),
            _const_spec((1, D_A)),
            _const_spec((1, D_A)),
            _const_spec((8, D_B)),
            _const_spec((D_A + D_B, d)),
        ],
        out_specs=pl.BlockSpec((1, tm, d), lambda b, j: (b, j, 0)),
        out_shape=jax.ShapeDtypeStruct(x.shape, x.dtype),
        scratch_shapes=[
            pltpu.VMEM((A_HIST + tm, D_A), jnp.float32),
            pltpu.VMEM((B_HIST + tm, D_B), jnp.float32),
        ],
        compiler_params=pltpu.CompilerParams(
            dimension_semantics=("arbitrary", "arbitrary"),
            vmem_limit_bytes=VMEM_LIMIT_BYTES),
    )(x, row(g), w_in, caw, row(conv_a_b), row(ln_g), row(ln_b), cbw, w_out)


def _pool_mixer_kernel(x_ref, g_ref, pw_ref, ps_ref, o_ref, hbuf):
    tm = x_ref.shape[1]
    j = pl.program_id(1)

    @pl.when(j == 0)
    def _():
        hbuf[0:POOL_HIST, :] = jnp.zeros((POOL_HIST, D_MODEL), jnp.float32)

    x = x_ref[0]
    hbuf[POOL_HIST:POOL_HIST + tm, :] = _rms_scale(x) * g_ref[...]
    pos = (j * tm + 1 + lax.broadcasted_iota(jnp.int32, (tm, 1), 0)).astype(jnp.float32)

    for grp, win in enumerate(POOL_WINDOWS):
        lo = grp * POOL_GROUP_DIM
        cols = slice(lo, lo + POOL_GROUP_DIM)
        hg = hbuf[POOL_HIST:POOL_HIST + tm, cols]
        s = hg
        for lag in range(1, win):
            s = s + hbuf[POOL_HIST - lag:POOL_HIST - lag + tm, cols]
        pooled = s / jnp.minimum(pos, float(win)) - hg
        mixed = jnp.dot(pooled.astype(jnp.bfloat16), pw_ref[grp],
                        preferred_element_type=jnp.float32)
        o_ref[0, :, cols] = x[:, cols] + mixed * ps_ref[:, cols]

    hbuf[0:POOL_HIST, :] = hbuf[tm:tm + POOL_HIST, :]


def _pool_mixer(x, g, pool_w, pool_scale):
    bsz, seq, d = x.shape
    tm = TM_POOL
    return pl.pallas_call(
        _pool_mixer_kernel,
        name="pool_mixer",
        grid=(bsz, seq // tm),
        in_specs=[
            pl.BlockSpec((1, tm, d), lambda b, j: (b, j, 0)),
            _const_spec((1, d)),
            _const_spec(pool_w.shape),
            _const_spec((1, d)),
        ],
        out_specs=pl.BlockSpec((1, tm, d), lambda b, j: (b, j, 0)),
        out_shape=jax.ShapeDtypeStruct(x.shape, x.dtype),
        scratch_shapes=[pltpu.VMEM((POOL_HIST + tm, d), jnp.float32)],
        compiler_params=pltpu.CompilerParams(
            dimension_semantics=("arbitrary", "arbitrary"),
            vmem_limit_bytes=VMEM_LIMIT_BYTES),
    )(x, g.reshape(1, -1), pool_w, pool_scale.reshape(1, -1))


def _mlp_kernel(x_hbm, g_ref, w1_ref, w2_ref, fg_ref, o_ref, xbuf, h_ref, sem, *,
                apply_final_norm):
    i, f = pl.program_id(0), pl.program_id(1)
    tm = o_ref.shape[0]

    def x_copy(tile):
        return pltpu.make_async_copy(x_hbm.at[pl.ds(tile * tm, tm), :], xbuf, sem)

    @pl.when(f == 0)
    def _():
        @pl.when(i == 0)
        def _():
            x_copy(0).start()

        x_copy(i).wait()
        x = xbuf[...]
        h_ref[...] = (_rms_scale(x) * g_ref[...]).astype(jnp.bfloat16)
        o_ref[...] = x

    @pl.when((f == 1) & (i + 1 < pl.num_programs(0)))
    def _():
        x_copy(i + 1).start()

    w1 = w1_ref[...].astype(jnp.bfloat16)
    z = jnp.dot(h_ref[...], w1, preferred_element_type=jnp.float32)
    z = jnp.square(jnp.maximum(z, 0.0)).astype(jnp.bfloat16)
    w2 = w2_ref[...].astype(jnp.bfloat16)
    o_ref[...] += jnp.dot(z, w2, preferred_element_type=jnp.float32)

    if apply_final_norm:
        @pl.when(f == pl.num_programs(1) - 1)
        def _():
            o_ref[...] = _rms_scale(o_ref[...]) * fg_ref[...]


def _mlp(x2d, g, w1_all, w2_all, layer, final_g, apply_final_norm):
    n, d = x2d.shape
    tm, tf = TM_MLP, TF_MLP
    assert n % tm == 0 and D_FF % tf == 0 and D_FF // tf >= 2
    return pl.pallas_call(
        functools.partial(_mlp_kernel, apply_final_norm=apply_final_norm),
        name="mlp",
        grid=(n // tm, D_FF // tf),
        in_specs=[
            pl.BlockSpec(memory_space=pl.ANY),
            _const_spec((1, d)),
            pl.BlockSpec((None, d, tf), lambda i, f: (layer, 0, f)),
            pl.BlockSpec((None, tf, d), lambda i, f: (layer, f, 0)),
            _const_spec((1, d)),
        ],
        out_specs=pl.BlockSpec((tm, d), lambda i, f: (i, 0)),
        out_shape=jax.ShapeDtypeStruct(x2d.shape, x2d.dtype),
        scratch_shapes=[
            pltpu.VMEM((tm, d), jnp.float32),
            pltpu.VMEM((tm, d), jnp.bfloat16),
            pltpu.SemaphoreType.DMA(()),
        ],
        compiler_params=pltpu.CompilerParams(
            dimension_semantics=("arbitrary", "arbitrary"),
            vmem_limit_bytes=VMEM_LIMIT_BYTES),
    )(x2d, g.reshape(1, -1), w1_all, w2_all, final_g.reshape(1, -1))


def kernel(x, norm_mix_g, norm_mlp_g, w_in_even, conv_a_w, conv_a_b, ln_a_g, ln_a_b,
           conv_b_w, w_out_even, pool_w, pool_scale, mlp_w1, mlp_w2, final_g):
    bsz, seq, d = x.shape
    depth = norm_mix_g.shape[0]
    n_odd, n_groups, gdim, _ = pool_w.shape
    pool_w_rows = pool_w.reshape(n_odd, n_groups * gdim, gdim)
    for layer in range(depth):
        i = layer // 2
        if layer % 2 == 0:
            x = _even_mixer(x, norm_mix_g[layer], _layer_weight_bf16(w_in_even, i, 256),
                            conv_a_w[i], conv_a_b[i], ln_a_g[i], ln_a_b[i], conv_b_w[i],
                            _layer_weight_bf16(w_out_even, i, 512))
        else:
            pw = _layer_weight_bf16(pool_w_rows, i, n_groups * gdim)
            x = _pool_mixer(x, norm_mix_g[layer], pw.reshape(n_groups, gdim, gdim), pool_scale[i])
        x = _mlp(x.reshape(bsz * seq, d), norm_mlp_g[layer], mlp_w1, mlp_w2, layer, final_g,
                 apply_final_norm=(layer == depth - 1)).reshape(bsz, seq, d)
    return x
```

```python
import functools

import jax
import jax.numpy as jnp
from jax import lax
from jax.experimental import pallas as pl
from jax.experimental.pallas import tpu as pltpu

D_MODEL = 2048
D_A = D_MODEL // 2
D_B = D_MODEL // 2
D_IN_EVEN = 2 * D_A + 3 * D_B
CONV_A_WIDTH = 31
CONV_B_WIDTH = 3
POOL_WINDOWS = (2, 4, 8, 16)
POOL_GROUP_DIM = D_MODEL // len(POOL_WINDOWS)
D_FF = 4 * D_MODEL
RMS_EPS = 1e-6
LN_EPS = 1e-5

VMEM_LIMIT_BYTES = 56 * 1024 * 1024

A_HIST = 32
B_HIST = 8
POOL_HIST = 16

LANES = 128
ROW_PITCH = 2
CONV_ROWS = 128

TM_EVEN = 256
TM_POOL = 512
TM_MLP = 1024
TF_MLP = 512


def _rms_scale(x):
    ms = jnp.mean(x * x, axis=-1, keepdims=True)
    return x * lax.rsqrt(ms + RMS_EPS)


def _const_spec(shape):
    zeros = (0,) * len(shape)
    return pl.BlockSpec(shape, lambda *_: zeros, pipeline_mode=pl.Buffered(1))


def _cast_kernel(w_ref, o_ref):
    o_ref[...] = w_ref[...].astype(o_ref.dtype)


def _layer_weight_bf16(w_all, layer, block_rows):
    _, rows, cols = w_all.shape
    assert rows % block_rows == 0
    return pl.pallas_call(
        _cast_kernel,
        name="cast_bf16",
        grid=(rows // block_rows,),
        in_specs=[pl.BlockSpec((None, block_rows, cols), lambda r: (layer, r, 0))],
        out_specs=pl.BlockSpec((block_rows, cols), lambda r: (r, 0)),
        out_shape=jax.ShapeDtypeStruct((rows, cols), jnp.bfloat16),
        compiler_params=pltpu.CompilerParams(
            dimension_semantics=("arbitrary",),
            vmem_limit_bytes=VMEM_LIMIT_BYTES),
    )(w_all)


def _slab_rows(hist, tm):
    return ROW_PITCH * (hist + tm)


def _slab_window(buf, c, token, rows):
    return buf[c, pl.ds(ROW_PITCH * token, rows, stride=ROW_PITCH), :]


def _slab_store(buf, hist, value):
    tm = value.shape[0]
    for c in range(value.shape[1] // LANES):
        buf[c, pl.ds(ROW_PITCH * hist, tm, stride=ROW_PITCH), :] = value[:, c * LANES:(c + 1) * LANES]


def _slab_carry(buf, hist, tm, reset):
    for c in range(buf.shape[0]):
        tail = _slab_window(buf, c, tm, hist)
        buf[c, pl.ds(0, hist, stride=ROW_PITCH), :] = jnp.where(reset, 0.0, tail)


def _causal_depthwise_conv(buf, hist, w_ref, width, tm, bias_ref=None):
    outs = []
    for c in range(buf.shape[0]):
        lanes = slice(c * LANES, (c + 1) * LANES)
        parts = []
        for r0 in range(0, tm, CONV_ROWS):
            acc = None
            for k in range(width):
                window = _slab_window(buf, c, hist - (width - 1) + k + r0, CONV_ROWS)
                term = w_ref[k:k + 1, lanes] * window
                acc = term if acc is None else acc + term
            if bias_ref is not None:
                acc = acc + bias_ref[:, lanes]
            parts.append(acc)
        outs.append(jnp.concatenate(parts, axis=0))
    return jnp.concatenate(outs, axis=1)


def _even_mixer_kernel(x_ref, g_ref, win_ref, caw_ref, cab_ref, lng_ref, lnb_ref,
                       cbw_ref, wout_ref, o_ref, abuf, bbuf):
    tm = x_ref.shape[1]

    @pl.when((pl.program_id(0) == 0) & (pl.program_id(1) == 0))
    def _():
        abuf[...] = jnp.zeros(abuf.shape, jnp.float32)
        bbuf[...] = jnp.zeros(bbuf.shape, jnp.float32)

    sequence_start = pl.program_id(1) == 0
    _slab_carry(abuf, A_HIST, tm, sequence_start)
    _slab_carry(bbuf, B_HIST, tm, sequence_start)

    x = x_ref[0]
    h = (_rms_scale(x) * g_ref[...]).astype(jnp.bfloat16)

    def proj(lo, width):
        return jnp.dot(h, win_ref[:, lo:lo + width], preferred_element_type=jnp.float32)

    a_val = proj(0, D_A)
    a_gate = proj(D_A, D_A)
    _slab_store(abuf, A_HIST, a_val * jax.nn.sigmoid(a_gate))
    b_x = proj(2 * D_A, D_B)
    b_c = proj(2 * D_A + D_B, D_B)
    _slab_store(bbuf, B_HIST, b_c * b_x)
    b_b = proj(2 * D_A + 2 * D_B, D_B)

    acc = _causal_depthwise_conv(abuf, A_HIST, caw_ref, CONV_A_WIDTH, tm, cab_ref)
    mu = jnp.mean(acc, axis=-1, keepdims=True)
    xc = acc - mu
    var = jnp.mean(xc * xc, axis=-1, keepdims=True)
    y = xc * lax.rsqrt(var + LN_EPS) * lng_ref[...] + lnb_ref[...]
    a_out = y * jax.nn.sigmoid(y)

    bo = b_b * _causal_depthwise_conv(bbuf, B_HIST, cbw_ref, CONV_B_WIDTH, tm)

    cat = jnp.concatenate([a_out, bo], axis=-1).astype(jnp.bfloat16)
    o_ref[0] = x + jnp.dot(cat, wout_ref[...], preferred_element_type=jnp.float32)


def _even_mixer(x, g, w_in, conv_a_w, conv_a_b, ln_g, ln_b, conv_b_w, w_out):
    bsz, seq, d = x.shape
    tm = TM_EVEN
    caw = jnp.pad(conv_a_w, ((0, 32 - CONV_A_WIDTH), (0, 0)))
    cbw = jnp.pad(conv_b_w, ((0, 8 - CONV_B_WIDTH), (0, 0)))
    row = lambda v: v.reshape(1, -1)
    return pl.pallas_call(
        _even_mixer_kernel,
        name="even_mixer",
        grid=(bsz, seq // tm),
        in_specs=[
            pl.BlockSpec((1, tm, d), lambda b, j: (b, j, 0)),
            _const_spec((1, d)),
            _const_spec((d, D_IN_EVEN)),
            _const_spec((32, D_A)),
            _const_spec((1, D_A)),
            _const_spec((1, D_A)),
            _const_spec((1, D_A)),
            _const_spec((8, D_B)),
            _const_spec((D_A + D_B, d)),
        ],
        out_specs=pl.BlockSpec((1, tm, d), lambda b, j: (b, j, 0)),
        out_shape=jax.ShapeDtypeStruct(x.shape, x.dtype),
        scratch_shapes=[
            pltpu.VMEM((D_A // LANES, _slab_rows(A_HIST, tm), LANES), jnp.float32),
            pltpu.VMEM((D_B // LANES, _slab_rows(B_HIST, tm), LANES), jnp.float32),
        ],
        compiler_params=pltpu.CompilerParams(
            dimension_semantics=("arbitrary", "arbitrary"),
            vmem_limit_bytes=VMEM_LIMIT_BYTES),
    )(x, row(g), w_in, caw, row(conv_a_b), row(ln_g), row(ln_b), cbw, w_out)


def _pool_mixer_kernel(x_ref, g_ref, pw_ref, ps_ref, o_ref, hbuf):
    tm = x_ref.shape[1]
    j = pl.program_id(1)

    @pl.when((pl.program_id(0) == 0) & (j == 0))
    def _():
        hbuf[...] = jnp.zeros(hbuf.shape, jnp.float32)

    _slab_carry(hbuf, POOL_HIST, tm, j == 0)

    x = x_ref[0]
    h = _rms_scale(x) * g_ref[...]
    _slab_store(hbuf, POOL_HIST, h)
    pos = (j * tm + 1 + lax.broadcasted_iota(jnp.int32, (tm, 1), 0)).astype(jnp.float32)

    chunks_per_group = POOL_GROUP_DIM // LANES
    for grp, win in enumerate(POOL_WINDOWS):
        lo = grp * POOL_GROUP_DIM
        cols = slice(lo, lo + POOL_GROUP_DIM)
        sums = []
        for c in range(grp * chunks_per_group, (grp + 1) * chunks_per_group):
            s = h[:, c * LANES:(c + 1) * LANES]
            for lag in range(1, win):
                s = s + _slab_window(hbuf, c, POOL_HIST - lag, tm)
            sums.append(s)
        pooled = jnp.concatenate(sums, axis=1) / jnp.minimum(pos, float(win)) - h[:, cols]
        mixed = jnp.dot(pooled.astype(jnp.bfloat16), pw_ref[grp],
                        preferred_element_type=jnp.float32)
        o_ref[0, :, cols] = x[:, cols] + mixed * ps_ref[:, cols]


def _pool_mixer(x, g, pool_w, pool_scale):
    bsz, seq, d = x.shape
    tm = TM_POOL
    return pl.pallas_call(
        _pool_mixer_kernel,
        name="pool_mixer",
        grid=(bsz, seq // tm),
        in_specs=[
            pl.BlockSpec((1, tm, d), lambda b, j: (b, j, 0)),
            _const_spec((1, d)),
            _const_spec(pool_w.shape),
            _const_spec((1, d)),
        ],
        out_specs=pl.BlockSpec((1, tm, d), lambda b, j: (b, j, 0)),
        out_shape=jax.ShapeDtypeStruct(x.shape, x.dtype),
        scratch_shapes=[pltpu.VMEM((d // LANES, _slab_rows(POOL_HIST, tm), LANES), jnp.float32)],
        compiler_params=pltpu.CompilerParams(
            dimension_semantics=("arbitrary", "arbitrary"),
            vmem_limit_bytes=VMEM_LIMIT_BYTES),
    )(x, g.reshape(1, -1), pool_w, pool_scale.reshape(1, -1))


def _mlp_kernel(x_hbm, g_ref, w1_ref, w2_ref, fg_ref, o_ref, xbuf, h_ref, sem, *,
                apply_final_norm):
    i, f = pl.program_id(0), pl.program_id(1)
    tm = o_ref.shape[0]

    def x_copy(tile):
        return pltpu.make_async_copy(x_hbm.at[pl.ds(tile * tm, tm), :], xbuf, sem)

    @pl.when(f == 0)
    def _():
        @pl.when(i == 0)
        def _():
            x_copy(0).start()

        x_copy(i).wait()
        x = xbuf[...]
        h_ref[...] = (_rms_scale(x) * g_ref[...]).astype(jnp.bfloat16)
        o_ref[...] = x

    @pl.when((f == 1) & (i + 1 < pl.num_programs(0)))
    def _():
        x_copy(i + 1).start()

    w1 = w1_ref[...].astype(jnp.bfloat16)
    z = jnp.dot(h_ref[...], w1, preferred_element_type=jnp.float32)
    z = jnp.square(jnp.maximum(z, 0.0)).astype(jnp.bfloat16)
    w2 = w2_ref[...].astype(jnp.bfloat16)
    o_ref[...] += jnp.dot(z, w2, preferred_element_type=jnp.float32)

    if apply_final_norm:
        @pl.when(f == pl.num_programs(1) - 1)
        def _():
            o_ref[...] = _rms_scale(o_ref[...]) * fg_ref[...]


def _mlp(x2d, g, w1_all, w2_all, layer, final_g, apply_final_norm):
    n, d = x2d.shape
    tm, tf = TM_MLP, TF_MLP
    assert n % tm == 0 and D_FF % tf == 0 and D_FF // tf >= 2
    return pl.pallas_call(
        functools.partial(_mlp_kernel, apply_final_norm=apply_final_norm),
        name="mlp",
        grid=(n // tm, D_FF // tf),
        in_specs=[
            pl.BlockSpec(memory_space=pl.ANY),
            _const_spec((1, d)),
            pl.BlockSpec((None, d, tf), lambda i, f: (layer, 0, f)),
            pl.BlockSpec((None, tf, d), lambda i, f: (layer, f, 0)),
            _const_spec((1, d)),
        ],
        out_specs=pl.BlockSpec((tm, d), lambda i, f: (i, 0)),
        out_shape=jax.ShapeDtypeStruct(x2d.shape, x2d.dtype),
        scratch_shapes=[
            pltpu.VMEM((tm, d), jnp.float32),
            pltpu.VMEM((tm, d), jnp.bfloat16),
            pltpu.SemaphoreType.DMA(()),
        ],
        compiler_params=pltpu.CompilerParams(
            dimension_semantics=("arbitrary", "arbitrary"),
            vmem_limit_bytes=VMEM_LIMIT_BYTES),
    )(x2d, g.reshape(1, -1), w1_all, w2_all, final_g.reshape(1, -1))


def kernel(x, norm_mix_g, norm_mlp_g, w_in_even, conv_a_w, conv_a_b, ln_a_g, ln_a_b,
           conv_b_w, w_out_even, pool_w, pool_scale, mlp_w1, mlp_w2, final_g):
    bsz, seq, d = x.shape
    depth = norm_mix_g.shape[0]
    n_odd, n_groups, gdim, _ = pool_w.shape
    pool_w_rows = pool_w.reshape(n_odd, n_groups * gdim, gdim)
    for layer in range(depth):
        i = layer // 2
        if layer % 2 == 0:
            x = _even_mixer(x, norm_mix_g[layer], _layer_weight_bf16(w_in_even, i, 256),
                            conv_a_w[i], conv_a_b[i], ln_a_g[i], ln_a_b[i], conv_b_w[i],
                            _layer_weight_bf16(w_out_even, i, 512))
        else:
            pw = _layer_weight_bf16(pool_w_rows, i, n_groups * gdim)
            x = _pool_mixer(x, norm_mix_g[layer], pw.reshape(n_groups, gdim, gdim), pool_scale[i])
        x = _mlp(x.reshape(bsz * seq, d), norm_mlp_g[layer], mlp_w1, mlp_w2, layer, final_g,
                 apply_final_norm=(layer == depth - 1)).reshape(bsz, seq, d)
    return x
```

```python
import functools

import jax
import jax.numpy as jnp
from jax import lax
from jax.experimental import pallas as pl
from jax.experimental.pallas import tpu as pltpu

D_MODEL = 2048
D_A = D_MODEL // 2
D_B = D_MODEL // 2
D_IN_EVEN = 2 * D_A + 3 * D_B
CONV_A_WIDTH = 31
CONV_B_WIDTH = 3
POOL_WINDOWS = (2, 4, 8, 16)
POOL_GROUP_DIM = D_MODEL // len(POOL_WINDOWS)
D_FF = 4 * D_MODEL
RMS_EPS = 1e-6
LN_EPS = 1e-5

VMEM_LIMIT_BYTES = 56 * 1024 * 1024

A_HIST = 32
B_HIST = 8
POOL_HIST = 16

LANES = 128
ROW_PITCH = 2
CONV_ROWS = 128

TM_EVEN = 256
TM_POOL = 512
TM_MLP = 1024
TF_MLP = 512
MLP_AHEAD_ROWS = 128


def _rms_scale(x):
    ms = jnp.mean(x * x, axis=-1, keepdims=True)
    return x * lax.rsqrt(ms + RMS_EPS)


def _const_spec(shape):
    zeros = (0,) * len(shape)
    return pl.BlockSpec(shape, lambda *_: zeros, pipeline_mode=pl.Buffered(1))


def _stacked_spec(stacked, index):
    _, rows, cols = stacked.shape
    return pl.BlockSpec((None, rows, cols), lambda *_: (index, 0, 0), pipeline_mode=pl.Buffered(1))


def _cast_kernel(w_ref, o_ref):
    o_ref[...] = w_ref[...].astype(o_ref.dtype)


def _layer_weight_bf16(w_all, layer, block_rows):
    _, rows, cols = w_all.shape
    assert rows % block_rows == 0
    return pl.pallas_call(
        _cast_kernel,
        name="cast_bf16",
        grid=(rows // block_rows,),
        in_specs=[pl.BlockSpec((None, block_rows, cols), lambda r: (layer, r, 0))],
        out_specs=pl.BlockSpec((block_rows, cols), lambda r: (r, 0)),
        out_shape=jax.ShapeDtypeStruct((rows, cols), jnp.bfloat16),
        compiler_params=pltpu.CompilerParams(
            dimension_semantics=("arbitrary",),
            vmem_limit_bytes=VMEM_LIMIT_BYTES),
    )(w_all)


def _slab_rows(hist, tm):
    return ROW_PITCH * (hist + tm)


def _slab_window(buf, c, token, rows):
    return buf[c, pl.ds(ROW_PITCH * token, rows, stride=ROW_PITCH), :]


def _slab_store(buf, hist, value):
    tm = value.shape[0]
    for c in range(value.shape[1] // LANES):
        buf[c, pl.ds(ROW_PITCH * hist, tm, stride=ROW_PITCH), :] = value[:, c * LANES:(c + 1) * LANES]


def _slab_carry(buf, hist, tm, reset):
    for c in range(buf.shape[0]):
        tail = _slab_window(buf, c, tm, hist)
        buf[c, pl.ds(0, hist, stride=ROW_PITCH), :] = jnp.where(reset, 0.0, tail)


def _causal_depthwise_conv(buf, hist, w_ref, width, tm, bias_ref=None):
    outs = []
    for c in range(buf.shape[0]):
        lanes = slice(c * LANES, (c + 1) * LANES)
        parts = []
        for r0 in range(0, tm, CONV_ROWS):
            acc = None
            for k in range(width):
                window = _slab_window(buf, c, hist - (width - 1) + k + r0, CONV_ROWS)
                term = w_ref[k:k + 1, lanes] * window
                acc = term if acc is None else acc + term
            if bias_ref is not None:
                acc = acc + bias_ref[:, lanes]
            parts.append(acc)
        outs.append(jnp.concatenate(parts, axis=0))
    return jnp.concatenate(outs, axis=1)


def _even_mixer_kernel(x_ref, g_ref, win_ref, caw_ref, cab_ref, lng_ref, lnb_ref,
                       cbw_ref, wout_ref, o_ref, abuf, bbuf):
    tm = x_ref.shape[1]

    @pl.when((pl.program_id(0) == 0) & (pl.program_id(1) == 0))
    def _():
        abuf[...] = jnp.zeros(abuf.shape, jnp.float32)
        bbuf[...] = jnp.zeros(bbuf.shape, jnp.float32)

    sequence_start = pl.program_id(1) == 0
    _slab_carry(abuf, A_HIST, tm, sequence_start)
    _slab_carry(bbuf, B_HIST, tm, sequence_start)

    x = x_ref[0]
    h = (_rms_scale(x) * g_ref[...]).astype(jnp.bfloat16)

    def proj(lo, width):
        return jnp.dot(h, win_ref[:, lo:lo + width], preferred_element_type=jnp.float32)

    a_val = proj(0, D_A)
    a_gate = proj(D_A, D_A)
    _slab_store(abuf, A_HIST, a_val * jax.nn.sigmoid(a_gate))
    b_x = proj(2 * D_A, D_B)
    b_c = proj(2 * D_A + D_B, D_B)
    _slab_store(bbuf, B_HIST, b_c * b_x)
    b_b = proj(2 * D_A + 2 * D_B, D_B)

    acc = _causal_depthwise_conv(abuf, A_HIST, caw_ref, CONV_A_WIDTH, tm, cab_ref)
    mu = jnp.mean(acc, axis=-1, keepdims=True)
    xc = acc - mu
    var = jnp.mean(xc * xc, axis=-1, keepdims=True)
    y = xc * lax.rsqrt(var + LN_EPS) * lng_ref[...] + lnb_ref[...]
    a_out = y * jax.nn.sigmoid(y)

    bo = b_b * _causal_depthwise_conv(bbuf, B_HIST, cbw_ref, CONV_B_WIDTH, tm)

    cat = jnp.concatenate([a_out, bo], axis=-1).astype(jnp.bfloat16)
    o_ref[0] = x + jnp.dot(cat, wout_ref[...], preferred_element_type=jnp.float32)


def _even_mixer(x, g_all, layer, w_in, conv_a_w, conv_a_b, ln_g, ln_b, conv_b_w, w_out, i):
    bsz, seq, d = x.shape
    tm = TM_EVEN
    return pl.pallas_call(
        _even_mixer_kernel,
        name="even_mixer",
        grid=(bsz, seq // tm),
        in_specs=[
            pl.BlockSpec((1, tm, d), lambda b, j: (b, j, 0)),
            _stacked_spec(g_all, layer),
            _const_spec((d, D_IN_EVEN)),
            _stacked_spec(conv_a_w, i),
            _stacked_spec(conv_a_b, i),
            _stacked_spec(ln_g, i),
            _stacked_spec(ln_b, i),
            _stacked_spec(conv_b_w, i),
            _const_spec((D_A + D_B, d)),
        ],
        out_specs=pl.BlockSpec((1, tm, d), lambda b, j: (b, j, 0)),
        out_shape=jax.ShapeDtypeStruct(x.shape, x.dtype),
        scratch_shapes=[
            pltpu.VMEM((D_A // LANES, _slab_rows(A_HIST, tm), LANES), jnp.float32),
            pltpu.VMEM((D_B // LANES, _slab_rows(B_HIST, tm), LANES), jnp.float32),
        ],
        compiler_params=pltpu.CompilerParams(
            dimension_semantics=("arbitrary", "arbitrary"),
            vmem_limit_bytes=VMEM_LIMIT_BYTES),
    )(x, g_all, w_in, conv_a_w, conv_a_b, ln_g, ln_b, conv_b_w, w_out)


def _pool_mixer_kernel(x_ref, g_ref, pw_ref, ps_ref, o_ref, hbuf):
    tm = x_ref.shape[1]
    j = pl.program_id(1)

    @pl.when((pl.program_id(0) == 0) & (j == 0))
    def _():
        hbuf[...] = jnp.zeros(hbuf.shape, jnp.float32)

    _slab_carry(hbuf, POOL_HIST, tm, j == 0)

    x = x_ref[0]
    h = _rms_scale(x) * g_ref[...]
    _slab_store(hbuf, POOL_HIST, h)
    pos = (j * tm + 1 + lax.broadcasted_iota(jnp.int32, (tm, 1), 0)).astype(jnp.float32)

    chunks_per_group = POOL_GROUP_DIM // LANES
    for grp, win in enumerate(POOL_WINDOWS):
        lo = grp * POOL_GROUP_DIM
        cols = slice(lo, lo + POOL_GROUP_DIM)
        sums = []
        for c in range(grp * chunks_per_group, (grp + 1) * chunks_per_group):
            s = h[:, c * LANES:(c + 1) * LANES]
            for lag in range(1, win):
                s = s + _slab_window(hbuf, c, POOL_HIST - lag, tm)
            sums.append(s)
        pooled = jnp.concatenate(sums, axis=1) / jnp.minimum(pos, float(win)) - h[:, cols]
        mixed = jnp.dot(pooled.astype(jnp.bfloat16), pw_ref[grp],
                        preferred_element_type=jnp.float32)
        o_ref[0, :, cols] = x[:, cols] + mixed * ps_ref[:, cols]


def _pool_mixer(x, g_all, layer, pool_w, pool_scale_all, i):
    bsz, seq, d = x.shape
    tm = TM_POOL
    return pl.pallas_call(
        _pool_mixer_kernel,
        name="pool_mixer",
        grid=(bsz, seq // tm),
        in_specs=[
            pl.BlockSpec((1, tm, d), lambda b, j: (b, j, 0)),
            _stacked_spec(g_all, layer),
            _const_spec(pool_w.shape),
            _stacked_spec(pool_scale_all, i),
        ],
        out_specs=pl.BlockSpec((1, tm, d), lambda b, j: (b, j, 0)),
        out_shape=jax.ShapeDtypeStruct(x.shape, x.dtype),
        scratch_shapes=[pltpu.VMEM((d // LANES, _slab_rows(POOL_HIST, tm), LANES), jnp.float32)],
        compiler_params=pltpu.CompilerParams(
            dimension_semantics=("arbitrary", "arbitrary"),
            vmem_limit_bytes=VMEM_LIMIT_BYTES),
    )(x, g_all, pool_w, pool_scale_all)


def _mlp_kernel(x_hbm, g_ref, w1_ref, w2_ref, fg_ref, o_ref, xbuf, h_ref, sem, *,
                apply_final_norm):
    i, f = pl.program_id(0), pl.program_id(1)
    n_i, n_f = pl.num_programs(0), pl.num_programs(1)
    tm = o_ref.shape[0]
    slot = i % 2
    has_next = i + 1 < n_i
    first_ahead = n_f - tm // MLP_AHEAD_ROWS

    def x_copy(tile):
        return pltpu.make_async_copy(x_hbm.at[pl.ds(tile * tm, tm), :], xbuf, sem)

    def normed_rows(r0, rows):
        return (_rms_scale(xbuf[pl.ds(r0, rows), :]) * g_ref[...]).astype(jnp.bfloat16)

    @pl.when((i == 0) & (f == 0))
    def _():
        x_copy(0).start()
        x_copy(0).wait()
        h_ref[0] = normed_rows(0, tm)

    def step(init_from_x, prepare_next):
        w1 = w1_ref[...].astype(jnp.bfloat16)
        z = jnp.dot(h_ref[slot], w1, preferred_element_type=jnp.float32)
        z = jnp.square(jnp.maximum(z, 0.0)).astype(jnp.bfloat16)
        w2 = w2_ref[...].astype(jnp.bfloat16)
        contrib = jnp.dot(z, w2, preferred_element_type=jnp.float32)
        if init_from_x:
            o_ref[...] = xbuf[...] + contrib
        else:
            o_ref[...] += contrib
        if prepare_next:
            r0 = pl.multiple_of((f - first_ahead) * MLP_AHEAD_ROWS, MLP_AHEAD_ROWS)
            h_ref[1 - slot, pl.ds(r0, MLP_AHEAD_ROWS), :] = normed_rows(r0, MLP_AHEAD_ROWS)

    @pl.when(f == 0)
    def _():
        step(True, False)

    @pl.when((f == 1) & has_next)
    def _():
        x_copy(i + 1).start()

    @pl.when((f > 0) & (f < first_ahead))
    def _():
        step(False, False)

    @pl.when((f == first_ahead) & has_next)
    def _():
        x_copy(i + 1).wait()

    @pl.when(f >= first_ahead)
    def _():
        step(False, True)

    if apply_final_norm:
        @pl.when(f == pl.num_programs(1) - 1)
        def _():
            o_ref[...] = _rms_scale(o_ref[...]) * fg_ref[...]


def _mlp(x2d, g_all, w1_all, w2_all, layer, final_g, apply_final_norm):
    n, d = x2d.shape
    tm, tf = TM_MLP, TF_MLP
    assert n % tm == 0 and D_FF % tf == 0 and tm % MLP_AHEAD_ROWS == 0
    assert D_FF // tf - tm // MLP_AHEAD_ROWS >= 2
    return pl.pallas_call(
        functools.partial(_mlp_kernel, apply_final_norm=apply_final_norm),
        name="mlp",
        grid=(n // tm, D_FF // tf),
        in_specs=[
            pl.BlockSpec(memory_space=pl.ANY),
            _stacked_spec(g_all, layer),
            pl.BlockSpec((None, d, tf), lambda i, f: (layer, 0, f)),
            pl.BlockSpec((None, tf, d), lambda i, f: (layer, f, 0)),
            _const_spec((1, d)),
        ],
        out_specs=pl.BlockSpec((tm, d), lambda i, f: (i, 0)),
        out_shape=jax.ShapeDtypeStruct(x2d.shape, x2d.dtype),
        scratch_shapes=[
            pltpu.VMEM((tm, d), jnp.float32),
            pltpu.VMEM((2, tm, d), jnp.bfloat16),
            pltpu.SemaphoreType.DMA(()),
        ],
        compiler_params=pltpu.CompilerParams(
            dimension_semantics=("arbitrary", "arbitrary"),
            vmem_limit_bytes=VMEM_LIMIT_BYTES),
    )(x2d, g_all, w1_all, w2_all, final_g.reshape(1, -1))


def kernel(x, norm_mix_g, norm_mlp_g, w_in_even, conv_a_w, conv_a_b, ln_a_g, ln_a_b,
           conv_b_w, w_out_even, pool_w, pool_scale, mlp_w1, mlp_w2, final_g):
    bsz, seq, d = x.shape
    depth = norm_mix_g.shape[0]
    n_odd, n_groups, gdim, _ = pool_w.shape
    pool_w_rows = pool_w.reshape(n_odd, n_groups * gdim, gdim)
    rows = lambda stacked: stacked.reshape(stacked.shape[0], 1, stacked.shape[1])
    mix_g, mlp_g = rows(norm_mix_g), rows(norm_mlp_g)
    conv_a_b, ln_a_g, ln_a_b, pool_scale = rows(conv_a_b), rows(ln_a_g), rows(ln_a_b), rows(pool_scale)
    for layer in range(depth):
        i = layer // 2
        if layer % 2 == 0:
            x = _even_mixer(x, mix_g, layer, _layer_weight_bf16(w_in_even, i, 256),
                            conv_a_w, conv_a_b, ln_a_g, ln_a_b, conv_b_w,
                            _layer_weight_bf16(w_out_even, i, 512), i)
        else:
            pw = _layer_weight_bf16(pool_w_rows, i, n_groups * gdim)
            x = _pool_mixer(x, mix_g, layer, pw.reshape(n_groups, gdim, gdim), pool_scale, i)
        x = _mlp(x.reshape(bsz * seq, d), mlp_g, mlp_w1, mlp_w2, layer, final_g,
                 apply_final_norm=(layer == depth - 1)).reshape(bsz, seq, d)
    return x
```

```python
import functools

import jax
import jax.numpy as jnp
from jax import lax
from jax.experimental import pallas as pl
from jax.experimental.pallas import tpu as pltpu

D_MODEL = 2048
D_A = D_MODEL // 2
D_B = D_MODEL // 2
D_IN_EVEN = 2 * D_A + 3 * D_B
CONV_A_WIDTH = 31
CONV_B_WIDTH = 3
POOL_WINDOWS = (2, 4, 8, 16)
POOL_GROUP_DIM = D_MODEL // len(POOL_WINDOWS)
D_FF = 4 * D_MODEL
RMS_EPS = 1e-6
LN_EPS = 1e-5

VMEM_LIMIT_BYTES = 56 * 1024 * 1024

A_HIST = 32
B_HIST = 8
POOL_HIST = 16

LANES = 128
ROW_PITCH = 2
CONV_ROWS = 128
RELEASE_LAG = 3

TM_EVEN = 256
TM_POOL = 512
TM_MLP = 1024
TF_MLP = 512


def _rms_scale(x):
    ms = jnp.mean(x * x, axis=-1, keepdims=True)
    return x * lax.rsqrt(ms + RMS_EPS)


def _const_spec(shape):
    zeros = (0,) * len(shape)
    return pl.BlockSpec(shape, lambda *_: zeros, pipeline_mode=pl.Buffered(1))


def _stacked_spec(stacked, index):
    _, rows, cols = stacked.shape
    return pl.BlockSpec((None, rows, cols), lambda *_: (index, 0, 0), pipeline_mode=pl.Buffered(1))


def _cast_kernel(w_ref, o_ref):
    o_ref[...] = w_ref[...].astype(o_ref.dtype)


def _layer_weight_bf16(w_all, layer, block_rows):
    _, rows, cols = w_all.shape
    assert rows % block_rows == 0
    return pl.pallas_call(
        _cast_kernel,
        name="cast_bf16",
        grid=(rows // block_rows,),
        in_specs=[pl.BlockSpec((None, block_rows, cols), lambda r: (layer, r, 0))],
        out_specs=pl.BlockSpec((block_rows, cols), lambda r: (r, 0)),
        out_shape=jax.ShapeDtypeStruct((rows, cols), jnp.bfloat16),
        compiler_params=pltpu.CompilerParams(
            dimension_semantics=("arbitrary",),
            vmem_limit_bytes=VMEM_LIMIT_BYTES),
    )(w_all)


def _slab_rows(hist, tm):
    return ROW_PITCH * (hist + tm)


def _slab_window(buf, c, token, rows):
    return buf[c, pl.ds(ROW_PITCH * token, rows, stride=ROW_PITCH), :]


def _slab_store(buf, hist, value):
    for c in range(value.shape[1] // LANES):
        _slab_store_chunk(buf, hist, c, value[:, c * LANES:(c + 1) * LANES])


def _slab_store_chunk(buf, hist, c, value):
    buf[c, pl.ds(ROW_PITCH * hist, value.shape[0], stride=ROW_PITCH), :] = value


def _slab_carry(buf, hist, tm, reset):
    for c in range(buf.shape[0]):
        tail = _slab_window(buf, c, tm, hist)
        buf[c, pl.ds(0, hist, stride=ROW_PITCH), :] = jnp.where(reset, 0.0, tail)


def _causal_depthwise_conv(buf, hist, w_ref, width, tm, bias_ref=None):
    outs = []
    for c in range(buf.shape[0]):
        lanes = slice(c * LANES, (c + 1) * LANES)
        parts = []
        for r0 in range(0, tm, CONV_ROWS):
            acc = None
            for k in range(width):
                window = _slab_window(buf, c, hist - (width - 1) + k + r0, CONV_ROWS)
                term = w_ref[k:k + 1, lanes] * window
                acc = term if acc is None else acc + term
            if bias_ref is not None:
                acc = acc + bias_ref[:, lanes]
            parts.append(acc)
        outs.append(jnp.concatenate(parts, axis=0))
    return jnp.concatenate(outs, axis=1)


def _even_mixer_kernel(x_ref, g_ref, win_ref, caw_ref, cab_ref, lng_ref, lnb_ref,
                       cbw_ref, wout_ref, o_ref, abuf, bbuf):
    tm = x_ref.shape[1]

    @pl.when((pl.program_id(0) == 0) & (pl.program_id(1) == 0))
    def _():
        abuf[...] = jnp.zeros(abuf.shape, jnp.float32)
        bbuf[...] = jnp.zeros(bbuf.shape, jnp.float32)

    sequence_start = pl.program_id(1) == 0
    _slab_carry(abuf, A_HIST, tm, sequence_start)
    _slab_carry(bbuf, B_HIST, tm, sequence_start)

    x = x_ref[0]
    h = (_rms_scale(x) * g_ref[...]).astype(jnp.bfloat16)

    def proj(lo, width):
        return jnp.dot(h, win_ref[:, lo:lo + width], preferred_element_type=jnp.float32)

    piece = 2 * LANES
    n_pieces = D_A // piece
    chunks_per_piece = piece // LANES
    order = [(name, q) for q in range(n_pieces) for name in ("a_val", "a_gate")]
    order += [(name, q) for q in range(n_pieces) for name in ("b_x", "b_c")]
    order += [("b_b", q) for q in range(n_pieces)]
    first_col = {"a_val": 0, "a_gate": D_A, "b_x": 2 * D_A, "b_c": 2 * D_A + D_B,
                 "b_b": 2 * D_A + 2 * D_B}
    res, glu_chunks = {}, {}
    for idx, (name, q) in enumerate(order):
        r = proj(first_col[name] + q * piece, piece)
        res[name, q] = r
        if name == "a_gate":
            glu = res["a_val", q] * jax.nn.sigmoid(r)
            for half in range(chunks_per_piece):
                glu_chunks[q * chunks_per_piece + half] = glu[:, half * LANES:(half + 1) * LANES]
        if name == "b_c":
            cx = r * res["b_x", q]
            for half in range(chunks_per_piece):
                _slab_store_chunk(bbuf, B_HIST, q * chunks_per_piece + half,
                                  cx[:, half * LANES:(half + 1) * LANES])
        c = idx - RELEASE_LAG
        if 0 <= c < D_A // LANES:
            released = jnp.where(r[:, :LANES] > jnp.inf, 0.0, glu_chunks[c])
            _slab_store_chunk(abuf, A_HIST, c, released)
    b_b = jnp.concatenate([res["b_b", q] for q in range(n_pieces)], axis=1)

    acc = _causal_depthwise_conv(abuf, A_HIST, caw_ref, CONV_A_WIDTH, tm, cab_ref)
    mu = jnp.mean(acc, axis=-1, keepdims=True)
    xc = acc - mu
    var = jnp.mean(xc * xc, axis=-1, keepdims=True)
    y = xc * lax.rsqrt(var + LN_EPS) * lng_ref[...] + lnb_ref[...]
    a_out = y * jax.nn.sigmoid(y)

    bo = b_b * _causal_depthwise_conv(bbuf, B_HIST, cbw_ref, CONV_B_WIDTH, tm)

    cat = jnp.concatenate([a_out, bo], axis=-1).astype(jnp.bfloat16)
    o_ref[0] = x + jnp.dot(cat, wout_ref[...], preferred_element_type=jnp.float32)


def _even_mixer(x, g_all, layer, w_in, conv_a_w, conv_a_b, ln_g, ln_b, conv_b_w, w_out, i):
    bsz, seq, d = x.shape
    tm = TM_EVEN
    return pl.pallas_call(
        _even_mixer_kernel,
        name="even_mixer",
        grid=(bsz, seq // tm),
        in_specs=[
            pl.BlockSpec((1, tm, d), lambda b, j: (b, j, 0)),
            _stacked_spec(g_all, layer),
            _const_spec((d, D_IN_EVEN)),
            _stacked_spec(conv_a_w, i),
            _stacked_spec(conv_a_b, i),
            _stacked_spec(ln_g, i),
            _stacked_spec(ln_b, i),
            _stacked_spec(conv_b_w, i),
            _const_spec((D_A + D_B, d)),
        ],
        out_specs=pl.BlockSpec((1, tm, d), lambda b, j: (b, j, 0)),
        out_shape=jax.ShapeDtypeStruct(x.shape, x.dtype),
        scratch_shapes=[
            pltpu.VMEM((D_A // LANES, _slab_rows(A_HIST, tm), LANES), jnp.float32),
            pltpu.VMEM((D_B // LANES, _slab_rows(B_HIST, tm), LANES), jnp.float32),
        ],
        compiler_params=pltpu.CompilerParams(
            dimension_semantics=("arbitrary", "arbitrary"),
            vmem_limit_bytes=VMEM_LIMIT_BYTES),
    )(x, g_all, w_in, conv_a_w, conv_a_b, ln_g, ln_b, conv_b_w, w_out)


def _pool_mixer_kernel(x_ref, g_ref, pw_ref, ps_ref, o_ref, hbuf):
    tm = x_ref.shape[1]
    j = pl.program_id(1)

    @pl.when((pl.program_id(0) == 0) & (j == 0))
    def _():
        hbuf[...] = jnp.zeros(hbuf.shape, jnp.float32)

    _slab_carry(hbuf, POOL_HIST, tm, j == 0)

    x = x_ref[0]
    h = _rms_scale(x) * g_ref[...]
    _slab_store(hbuf, POOL_HIST, h)
    pos = (j * tm + 1 + lax.broadcasted_iota(jnp.int32, (tm, 1), 0)).astype(jnp.float32)

    chunks_per_group = POOL_GROUP_DIM // LANES
    for grp, win in enumerate(POOL_WINDOWS):
        lo = grp * POOL_GROUP_DIM
        cols = slice(lo, lo + POOL_GROUP_DIM)
        sums = []
        for c in range(grp * chunks_per_group, (grp + 1) * chunks_per_group):
            s = h[:, c * LANES:(c + 1) * LANES]
            for lag in range(1, win):
                s = s + _slab_window(hbuf, c, POOL_HIST - lag, tm)
            sums.append(s)
        pooled = jnp.concatenate(sums, axis=1) / jnp.minimum(pos, float(win)) - h[:, cols]
        mixed = jnp.dot(pooled.astype(jnp.bfloat16), pw_ref[grp],
                        preferred_element_type=jnp.float32)
        o_ref[0, :, cols] = x[:, cols] + mixed * ps_ref[:, cols]


def _pool_mixer(x, g_all, layer, pool_w, pool_scale_all, i):
    bsz, seq, d = x.shape
    tm = TM_POOL
    return pl.pallas_call(
        _pool_mixer_kernel,
        name="pool_mixer",
        grid=(bsz, seq // tm),
        in_specs=[
            pl.BlockSpec((1, tm, d), lambda b, j: (b, j, 0)),
            _stacked_spec(g_all, layer),
            _const_spec(pool_w.shape),
            _stacked_spec(pool_scale_all, i),
        ],
        out_specs=pl.BlockSpec((1, tm, d), lambda b, j: (b, j, 0)),
        out_shape=jax.ShapeDtypeStruct(x.shape, x.dtype),
        scratch_shapes=[pltpu.VMEM((d // LANES, _slab_rows(POOL_HIST, tm), LANES), jnp.float32)],
        compiler_params=pltpu.CompilerParams(
            dimension_semantics=("arbitrary", "arbitrary"),
            vmem_limit_bytes=VMEM_LIMIT_BYTES),
    )(x, g_all, pool_w, pool_scale_all)


def _mlp_kernel(x_hbm, g_ref, w1_ref, w2_ref, fg_ref, o_ref, xbuf, h_ref, sem, *,
                apply_final_norm):
    i, f = pl.program_id(0), pl.program_id(1)
    tm = o_ref.shape[0]

    def x_copy(tile):
        return pltpu.make_async_copy(x_hbm.at[pl.ds(tile * tm, tm), :], xbuf, sem)

    @pl.when(f == 0)
    def _():
        @pl.when(i == 0)
        def _():
            x_copy(0).start()

        x_copy(i).wait()
        x = xbuf[...]
        h_ref[...] = (_rms_scale(x) * g_ref[...]).astype(jnp.bfloat16)
        o_ref[...] = x

    @pl.when((f == 1) & (i + 1 < pl.num_programs(0)))
    def _():
        x_copy(i + 1).start()

    w1 = w1_ref[...].astype(jnp.bfloat16)
    z = jnp.dot(h_ref[...], w1, preferred_element_type=jnp.float32)
    z = jnp.square(jnp.maximum(z, 0.0)).astype(jnp.bfloat16)
    w2 = w2_ref[...].astype(jnp.bfloat16)
    o_ref[...] += jnp.dot(z, w2, preferred_element_type=jnp.float32)

    if apply_final_norm:
        @pl.when(f == pl.num_programs(1) - 1)
        def _():
            o_ref[...] = _rms_scale(o_ref[...]) * fg_ref[...]


def _mlp(x2d, g_all, w1_all, w2_all, layer, final_g, apply_final_norm):
    n, d = x2d.shape
    tm, tf = TM_MLP, TF_MLP
    assert n % tm == 0 and D_FF % tf == 0 and D_FF // tf >= 2
    return pl.pallas_call(
        functools.partial(_mlp_kernel, apply_final_norm=apply_final_norm),
        name="mlp",
        grid=(n // tm, D_FF // tf),
        in_specs=[
            pl.BlockSpec(memory_space=pl.ANY),
            _stacked_spec(g_all, layer),
            pl.BlockSpec((None, d, tf), lambda i, f: (layer, 0, f)),
            pl.BlockSpec((None, tf, d), lambda i, f: (layer, f, 0)),
            _const_spec((1, d)),
        ],
        out_specs=pl.BlockSpec((tm, d), lambda i, f: (i, 0)),
        out_shape=jax.ShapeDtypeStruct(x2d.shape, x2d.dtype),
        scratch_shapes=[
            pltpu.VMEM((tm, d), jnp.float32),
            pltpu.VMEM((tm, d), jnp.bfloat16),
            pltpu.SemaphoreType.DMA(()),
        ],
        compiler_params=pltpu.CompilerParams(
            dimension_semantics=("arbitrary", "arbitrary"),
            vmem_limit_bytes=VMEM_LIMIT_BYTES),
    )(x2d, g_all, w1_all, w2_all, final_g.reshape(1, -1))


def kernel(x, norm_mix_g, norm_mlp_g, w_in_even, conv_a_w, conv_a_b, ln_a_g, ln_a_b,
           conv_b_w, w_out_even, pool_w, pool_scale, mlp_w1, mlp_w2, final_g):
    bsz, seq, d = x.shape
    depth = norm_mix_g.shape[0]
    n_odd, n_groups, gdim, _ = pool_w.shape
    pool_w_rows = pool_w.reshape(n_odd, n_groups * gdim, gdim)
    rows = lambda stacked: stacked.reshape(stacked.shape[0], 1, stacked.shape[1])
    mix_g, mlp_g = rows(norm_mix_g), rows(norm_mlp_g)
    conv_a_b, ln_a_g, ln_a_b, pool_scale = rows(conv_a_b), rows(ln_a_g), rows(ln_a_b), rows(pool_scale)
    for layer in range(depth):
        i = layer // 2
        if layer % 2 == 0:
            x = _even_mixer(x, mix_g, layer, _layer_weight_bf16(w_in_even, i, 256),
                            conv_a_w, conv_a_b, ln_a_g, ln_a_b, conv_b_w,
                            _layer_weight_bf16(w_out_even, i, 512), i)
        else:
            pw = _layer_weight_bf16(pool_w_rows, i, n_groups * gdim)
            x = _pool_mixer(x, mix_g, layer, pw.reshape(n_groups, gdim, gdim), pool_scale, i)
        x = _mlp(x.reshape(bsz * seq, d), mlp_g, mlp_w1, mlp_w2, layer, final_g,
                 apply_final_norm=(layer == depth - 1)).reshape(bsz, seq, d)
    return x
```

```python
import functools

import jax
import jax.numpy as jnp
from jax import lax
from jax.experimental import pallas as pl
from jax.experimental.pallas import tpu as pltpu

D_MODEL = 2048
D_A = D_MODEL // 2
D_B = D_MODEL // 2
D_IN_EVEN = 2 * D_A + 3 * D_B
CONV_A_WIDTH = 31
CONV_B_WIDTH = 3
POOL_WINDOWS = (2, 4, 8, 16)
POOL_GROUP_DIM = D_MODEL // len(POOL_WINDOWS)
D_FF = 4 * D_MODEL
RMS_EPS = 1e-6
LN_EPS = 1e-5

VMEM_LIMIT_BYTES = 56 * 1024 * 1024

A_HIST = 32
B_HIST = 8
POOL_HIST = 16

LANES = 128
ROW_PITCH = 2
CONV_ROWS = 128
RELEASE_LAG = 3
WEIGHT_CHUNK_ROWS = 128

TM_EVEN = 256
TM_POOL = 512
TM_MLP = 1024
TF_MLP = 512


def _rms_scale(x):
    ms = jnp.mean(x * x, axis=-1, keepdims=True)
    return x * lax.rsqrt(ms + RMS_EPS)


def _const_spec(shape):
    zeros = (0,) * len(shape)
    return pl.BlockSpec(shape, lambda *_: zeros, pipeline_mode=pl.Buffered(1))


def _stacked_spec(stacked, index):
    _, rows, cols = stacked.shape
    return pl.BlockSpec((None, rows, cols), lambda *_: (index, 0, 0), pipeline_mode=pl.Buffered(1))


def _slab_rows(hist, tm):
    return ROW_PITCH * (hist + tm)


def _slab_window(buf, c, token, rows):
    return buf[c, pl.ds(ROW_PITCH * token, rows, stride=ROW_PITCH), :]


def _slab_store(buf, hist, value):
    for c in range(value.shape[1] // LANES):
        _slab_store_chunk(buf, hist, c, value[:, c * LANES:(c + 1) * LANES])


def _slab_store_chunk(buf, hist, c, value):
    buf[c, pl.ds(ROW_PITCH * hist, value.shape[0], stride=ROW_PITCH), :] = value


def _slab_carry(buf, hist, tm, reset):
    for c in range(buf.shape[0]):
        tail = _slab_window(buf, c, tm, hist)
        buf[c, pl.ds(0, hist, stride=ROW_PITCH), :] = jnp.where(reset, 0.0, tail)


def _causal_depthwise_conv(buf, hist, w_ref, width, tm, bias_ref=None):
    outs = []
    for c in range(buf.shape[0]):
        lanes = slice(c * LANES, (c + 1) * LANES)
        parts = []
        for r0 in range(0, tm, CONV_ROWS):
            acc = None
            for k in range(width):
                window = _slab_window(buf, c, hist - (width - 1) + k + r0, CONV_ROWS)
                term = w_ref[k:k + 1, lanes] * window
                acc = term if acc is None else acc + term
            if bias_ref is not None:
                acc = acc + bias_ref[:, lanes]
            parts.append(acc)
        outs.append(jnp.concatenate(parts, axis=0))
    return jnp.concatenate(outs, axis=1)


def _fetch_weight_bf16(w_hbm, layer, w_vmem, rows_per_chunk):
    k, n = w_vmem.shape
    assert k % rows_per_chunk == 0
    n_chunks = k // rows_per_chunk

    def body(stage, sem):
        def chunk_copy(c, slot):
            src = w_hbm.at[layer, pl.ds(c * rows_per_chunk, rows_per_chunk), :]
            return pltpu.make_async_copy(src, stage.at[slot], sem.at[slot])

        chunk_copy(0, 0).start()

        @pl.loop(0, n_chunks)
        def _(c):
            slot = c % 2
            chunk_copy(c, slot).wait()

            @pl.when(c + 1 < n_chunks)
            def _():
                chunk_copy(c + 1, 1 - slot).start()

            r0 = pl.multiple_of(c * rows_per_chunk, rows_per_chunk)
            w_vmem[pl.ds(r0, rows_per_chunk), :] = stage[slot].astype(jnp.bfloat16)

    pl.run_scoped(body, pltpu.VMEM((2, rows_per_chunk, n), jnp.float32),
                  pltpu.SemaphoreType.DMA((2,)))


def _even_mixer_kernel(x_ref, g_ref, win_hbm, caw_ref, cab_ref, lng_ref, lnb_ref,
                       cbw_ref, wout_hbm, o_ref, abuf, bbuf, win_ref, wout_ref, *, weight_layer):
    tm = x_ref.shape[1]

    @pl.when((pl.program_id(0) == 0) & (pl.program_id(1) == 0))
    def _():
        _fetch_weight_bf16(win_hbm, weight_layer, win_ref, WEIGHT_CHUNK_ROWS)
        _fetch_weight_bf16(wout_hbm, weight_layer, wout_ref, WEIGHT_CHUNK_ROWS)
        abuf[...] = jnp.zeros(abuf.shape, jnp.float32)
        bbuf[...] = jnp.zeros(bbuf.shape, jnp.float32)

    sequence_start = pl.program_id(1) == 0
    _slab_carry(abuf, A_HIST, tm, sequence_start)
    _slab_carry(bbuf, B_HIST, tm, sequence_start)

    x = x_ref[0]
    h = (_rms_scale(x) * g_ref[...]).astype(jnp.bfloat16)

    def proj(lo, width):
        return jnp.dot(h, win_ref[:, lo:lo + width], preferred_element_type=jnp.float32)

    piece = 2 * LANES
    n_pieces = D_A // piece
    chunks_per_piece = piece // LANES
    order = [(name, q) for q in range(n_pieces) for name in ("a_val", "a_gate")]
    order += [(name, q) for q in range(n_pieces) for name in ("b_x", "b_c")]
    order += [("b_b", q) for q in range(n_pieces)]
    first_col = {"a_val": 0, "a_gate": D_A, "b_x": 2 * D_A, "b_c": 2 * D_A + D_B,
                 "b_b": 2 * D_A + 2 * D_B}
    res, glu_chunks = {}, {}
    for idx, (name, q) in enumerate(order):
        r = proj(first_col[name] + q * piece, piece)
        res[name, q] = r
        if name == "a_gate":
            glu = res["a_val", q] * jax.nn.sigmoid(r)
            for half in range(chunks_per_piece):
                glu_chunks[q * chunks_per_piece + half] = glu[:, half * LANES:(half + 1) * LANES]
        if name == "b_c":
            cx = r * res["b_x", q]
            for half in range(chunks_per_piece):
                _slab_store_chunk(bbuf, B_HIST, q * chunks_per_piece + half,
                                  cx[:, half * LANES:(half + 1) * LANES])
        c = idx - RELEASE_LAG
        if 0 <= c < D_A // LANES:
            released = jnp.where(r[:, :LANES] > jnp.inf, 0.0, glu_chunks[c])
            _slab_store_chunk(abuf, A_HIST, c, released)
    b_b = jnp.concatenate([res["b_b", q] for q in range(n_pieces)], axis=1)

    acc = _causal_depthwise_conv(abuf, A_HIST, caw_ref, CONV_A_WIDTH, tm, cab_ref)
    mu = jnp.mean(acc, axis=-1, keepdims=True)
    xc = acc - mu
    var = jnp.mean(xc * xc, axis=-1, keepdims=True)
    y = xc * lax.rsqrt(var + LN_EPS) * lng_ref[...] + lnb_ref[...]
    a_out = y * jax.nn.sigmoid(y)

    bo = b_b * _causal_depthwise_conv(bbuf, B_HIST, cbw_ref, CONV_B_WIDTH, tm)

    cat = jnp.concatenate([a_out, bo], axis=-1).astype(jnp.bfloat16)
    o_ref[0] = x + jnp.dot(cat, wout_ref[...], preferred_element_type=jnp.float32)


def _even_mixer(x, g_all, layer, w_in, conv_a_w, conv_a_b, ln_g, ln_b, conv_b_w, w_out, i):
    bsz, seq, d = x.shape
    tm = TM_EVEN
    return pl.pallas_call(
        functools.partial(_even_mixer_kernel, weight_layer=i),
        name="even_mixer",
        grid=(bsz, seq // tm),
        in_specs=[
            pl.BlockSpec((1, tm, d), lambda b, j: (b, j, 0)),
            _stacked_spec(g_all, layer),
            pl.BlockSpec(memory_space=pl.ANY),
            _stacked_spec(conv_a_w, i),
            _stacked_spec(conv_a_b, i),
            _stacked_spec(ln_g, i),
            _stacked_spec(ln_b, i),
            _stacked_spec(conv_b_w, i),
            pl.BlockSpec(memory_space=pl.ANY),
        ],
        out_specs=pl.BlockSpec((1, tm, d), lambda b, j: (b, j, 0)),
        out_shape=jax.ShapeDtypeStruct(x.shape, x.dtype),
        scratch_shapes=[
            pltpu.VMEM((D_A // LANES, _slab_rows(A_HIST, tm), LANES), jnp.float32),
            pltpu.VMEM((D_B // LANES, _slab_rows(B_HIST, tm), LANES), jnp.float32),
            pltpu.VMEM((d, D_IN_EVEN), jnp.bfloat16),
            pltpu.VMEM((D_A + D_B, d), jnp.bfloat16),
        ],
        compiler_params=pltpu.CompilerParams(
            dimension_semantics=("arbitrary", "arbitrary"),
            vmem_limit_bytes=VMEM_LIMIT_BYTES),
    )(x, g_all, w_in, conv_a_w, conv_a_b, ln_g, ln_b, conv_b_w, w_out)


def _pool_mixer_kernel(x_ref, g_ref, pw_hbm, ps_ref, o_ref, hbuf, pw_ref, *, weight_layer):
    tm = x_ref.shape[1]
    j = pl.program_id(1)

    @pl.when((pl.program_id(0) == 0) & (j == 0))
    def _():
        _fetch_weight_bf16(pw_hbm, weight_layer, pw_ref, POOL_GROUP_DIM)
        hbuf[...] = jnp.zeros(hbuf.shape, jnp.float32)

    _slab_carry(hbuf, POOL_HIST, tm, j == 0)

    x = x_ref[0]
    h = _rms_scale(x) * g_ref[...]
    _slab_store(hbuf, POOL_HIST, h)
    pos = (j * tm + 1 + lax.broadcasted_iota(jnp.int32, (tm, 1), 0)).astype(jnp.float32)

    chunks_per_group = POOL_GROUP_DIM // LANES
    for grp, win in enumerate(POOL_WINDOWS):
        lo = grp * POOL_GROUP_DIM
        cols = slice(lo, lo + POOL_GROUP_DIM)
        sums = []
        for c in range(grp * chunks_per_group, (grp + 1) * chunks_per_group):
            s = h[:, c * LANES:(c + 1) * LANES]
            for lag in range(1, win):
                s = s + _slab_window(hbuf, c, POOL_HIST - lag, tm)
            sums.append(s)
        pooled = jnp.concatenate(sums, axis=1) / jnp.minimum(pos, float(win)) - h[:, cols]
        mixed = jnp.dot(pooled.astype(jnp.bfloat16), pw_ref[lo:lo + POOL_GROUP_DIM, :],
                        preferred_element_type=jnp.float32)
        o_ref[0, :, cols] = x[:, cols] + mixed * ps_ref[:, cols]


def _pool_mixer(x, g_all, layer, pool_w_rows, pool_scale_all, i):
    bsz, seq, d = x.shape
    tm = TM_POOL
    return pl.pallas_call(
        functools.partial(_pool_mixer_kernel, weight_layer=i),
        name="pool_mixer",
        grid=(bsz, seq // tm),
        in_specs=[
            pl.BlockSpec((1, tm, d), lambda b, j: (b, j, 0)),
            _stacked_spec(g_all, layer),
            pl.BlockSpec(memory_space=pl.ANY),
            _stacked_spec(pool_scale_all, i),
        ],
        out_specs=pl.BlockSpec((1, tm, d), lambda b, j: (b, j, 0)),
        out_shape=jax.ShapeDtypeStruct(x.shape, x.dtype),
        scratch_shapes=[
            pltpu.VMEM((d // LANES, _slab_rows(POOL_HIST, tm), LANES), jnp.float32),
            pltpu.VMEM(pool_w_rows.shape[1:], jnp.bfloat16),
        ],
        compiler_params=pltpu.CompilerParams(
            dimension_semantics=("arbitrary", "arbitrary"),
            vmem_limit_bytes=VMEM_LIMIT_BYTES),
    )(x, g_all, pool_w_rows, pool_scale_all)


def _mlp_kernel(x_hbm, g_ref, w1_ref, w2_ref, fg_ref, o_ref, xbuf, h_ref, sem, *,
                apply_final_norm):
    i, f = pl.program_id(0), pl.program_id(1)
    tm = o_ref.shape[0]

    def x_copy(tile):
        return pltpu.make_async_copy(x_hbm.at[pl.ds(tile * tm, tm), :], xbuf, sem)

    @pl.when(f == 0)
    def _():
        @pl.when(i == 0)
        def _():
            x_copy(0).start()

        x_copy(i).wait()
        x = xbuf[...]
        h_ref[...] = (_rms_scale(x) * g_ref[...]).astype(jnp.bfloat16)
        o_ref[...] = x

    @pl.when((f == 1) & (i + 1 < pl.num_programs(0)))
    def _():
        x_copy(i + 1).start()

    w1 = w1_ref[...].astype(jnp.bfloat16)
    z = jnp.dot(h_ref[...], w1, preferred_element_type=jnp.float32)
    z = jnp.square(jnp.maximum(z, 0.0)).astype(jnp.bfloat16)
    w2 = w2_ref[...].astype(jnp.bfloat16)
    o_ref[...] += jnp.dot(z, w2, preferred_element_type=jnp.float32)

    if apply_final_norm:
        @pl.when(f == pl.num_programs(1) - 1)
        def _():
            o_ref[...] = _rms_scale(o_ref[...]) * fg_ref[...]


def _mlp(x2d, g_all, w1_all, w2_all, layer, final_g, apply_final_norm):
    n, d = x2d.shape
    tm, tf = TM_MLP, TF_MLP
    assert n % tm == 0 and D_FF % tf == 0 and D_FF // tf >= 2
    return pl.pallas_call(
        functools.partial(_mlp_kernel, apply_final_norm=apply_final_norm),
        name="mlp",
        grid=(n // tm, D_FF // tf),
        in_specs=[
            pl.BlockSpec(memory_space=pl.ANY),
            _stacked_spec(g_all, layer),
            pl.BlockSpec((None, d, tf), lambda i, f: (layer, 0, f)),
            pl.BlockSpec((None, tf, d), lambda i, f: (layer, f, 0)),
            _const_spec((1, d)),
        ],
        out_specs=pl.BlockSpec((tm, d), lambda i, f: (i, 0)),
        out_shape=jax.ShapeDtypeStruct(x2d.shape, x2d.dtype),
        scratch_shapes=[
            pltpu.VMEM((tm, d), jnp.float32),
            pltpu.VMEM((tm, d), jnp.bfloat16),
            pltpu.SemaphoreType.DMA(()),
        ],
        compiler_params=pltpu.CompilerParams(
            dimension_semantics=("arbitrary", "arbitrary"),
            vmem_limit_bytes=VMEM_LIMIT_BYTES),
    )(x2d, g_all, w1_all, w2_all, final_g.reshape(1, -1))


def kernel(x, norm_mix_g, norm_mlp_g, w_in_even, conv_a_w, conv_a_b, ln_a_g, ln_a_b,
           conv_b_w, w_out_even, pool_w, pool_scale, mlp_w1, mlp_w2, final_g):
    bsz, seq, d = x.shape
    depth = norm_mix_g.shape[0]
    n_odd, n_groups, gdim, _ = pool_w.shape
    pool_w_rows = pool_w.reshape(n_odd, n_groups * gdim, gdim)
    rows = lambda stacked: stacked.reshape(stacked.shape[0], 1, stacked.shape[1])
    mix_g, mlp_g = rows(norm_mix_g), rows(norm_mlp_g)
    conv_a_b, ln_a_g, ln_a_b, pool_scale = rows(conv_a_b), rows(ln_a_g), rows(ln_a_b), rows(pool_scale)
    for layer in range(depth):
        i = layer // 2
        if layer % 2 == 0:
            x = _even_mixer(x, mix_g, layer, w_in_even, conv_a_w, conv_a_b, ln_a_g, ln_a_b,
                            conv_b_w, w_out_even, i)
        else:
            x = _pool_mixer(x, mix_g, layer, pool_w_rows, pool_scale, i)
        x = _mlp(x.reshape(bsz * seq, d), mlp_g, mlp_w1, mlp_w2, layer, final_g,
                 apply_final_norm=(layer == depth - 1)).reshape(bsz, seq, d)
    return x
```

```python
import functools

import jax
import jax.numpy as jnp
from jax import lax
from jax.experimental import pallas as pl
from jax.experimental.pallas import tpu as pltpu

D_MODEL = 2048
D_A = D_MODEL // 2
D_B = D_MODEL // 2
D_IN_EVEN = 2 * D_A + 3 * D_B
CONV_A_WIDTH = 31
CONV_B_WIDTH = 3
POOL_WINDOWS = (2, 4, 8, 16)
POOL_GROUP_DIM = D_MODEL // len(POOL_WINDOWS)
D_FF = 4 * D_MODEL
RMS_EPS = 1e-6
LN_EPS = 1e-5

VMEM_LIMIT_BYTES = 56 * 1024 * 1024

A_HIST = 32
B_HIST = 8
POOL_HIST = 16

LANES = 128
ROW_PITCH = 2
CONV_ROWS = 128
RELEASE_LAG = 2
RELEASE_STRIDE = 2
WEIGHT_STAGE_SLOTS = 4
WEIGHT_CHUNK_BYTES = 3 << 19

TM_EVEN = 256
TM_POOL = 512
TM_MLP = 1024
TF_MLP = 512


def _rms_scale(x):
    ms = jnp.mean(x * x, axis=-1, keepdims=True)
    return x * lax.rsqrt(ms + RMS_EPS)


def _const_spec(shape):
    zeros = (0,) * len(shape)
    return pl.BlockSpec(shape, lambda *_: zeros, pipeline_mode=pl.Buffered(1))


def _stacked_spec(stacked, index):
    _, rows, cols = stacked.shape
    return pl.BlockSpec((None, rows, cols), lambda *_: (index, 0, 0), pipeline_mode=pl.Buffered(1))


def _slab_rows(hist, tm):
    return ROW_PITCH * (hist + tm)


def _slab_window(buf, c, token, rows):
    return buf[c, pl.ds(ROW_PITCH * token, rows, stride=ROW_PITCH), :]


def _slab_store(buf, hist, value):
    for c in range(value.shape[1] // LANES):
        _slab_store_chunk(buf, hist, c, value[:, c * LANES:(c + 1) * LANES])


def _slab_store_chunk(buf, hist, c, value):
    buf[c, pl.ds(ROW_PITCH * hist, value.shape[0], stride=ROW_PITCH), :] = value


def _slab_carry(buf, hist, tm, reset):
    for c in range(buf.shape[0]):
        tail = _slab_window(buf, c, tm, hist)
        buf[c, pl.ds(0, hist, stride=ROW_PITCH), :] = jnp.where(reset, 0.0, tail)


def _causal_depthwise_conv(buf, hist, w_ref, width, tm, bias_ref=None):
    outs = []
    for c in range(buf.shape[0]):
        lanes = slice(c * LANES, (c + 1) * LANES)
        parts = []
        for r0 in range(0, tm, CONV_ROWS):
            acc = None
            for k in range(width):
                window = _slab_window(buf, c, hist - (width - 1) + k + r0, CONV_ROWS)
                term = w_ref[k:k + 1, lanes] * window
                acc = term if acc is None else acc + term
            if bias_ref is not None:
                acc = acc + bias_ref[:, lanes]
            parts.append(acc)
        outs.append(jnp.concatenate(parts, axis=0))
    return jnp.concatenate(outs, axis=1)


def _fetch_weight_bf16(w_hbm, layer, w_vmem):
    k, n = w_vmem.shape
    rows_per_chunk = max(r for r in range(16, k + 1, 16)
                         if k % r == 0 and 4 * r * n <= WEIGHT_CHUNK_BYTES)
    n_chunks = k // rows_per_chunk
    ahead = WEIGHT_STAGE_SLOTS - 1
    assert n_chunks >= ahead

    def body(stage, sem):
        def chunk_copy(c):
            slot = c % WEIGHT_STAGE_SLOTS
            src = w_hbm.at[layer, pl.ds(c * rows_per_chunk, rows_per_chunk), :]
            return pltpu.make_async_copy(src, stage.at[slot], sem.at[slot])

        for c in range(ahead):
            chunk_copy(c).start()

        @pl.loop(0, n_chunks)
        def _(c):
            chunk_copy(c).wait()

            @pl.when(c + ahead < n_chunks)
            def _():
                chunk_copy(c + ahead).start()

            r0 = pl.multiple_of(c * rows_per_chunk, rows_per_chunk)
            w_vmem[pl.ds(r0, rows_per_chunk), :] = stage[c % WEIGHT_STAGE_SLOTS].astype(jnp.bfloat16)

    pl.run_scoped(body, pltpu.VMEM((WEIGHT_STAGE_SLOTS, rows_per_chunk, n), jnp.float32),
                  pltpu.SemaphoreType.DMA((WEIGHT_STAGE_SLOTS,)))


def _even_mixer_kernel(x_ref, g_ref, win_hbm, caw_ref, cab_ref, lng_ref, lnb_ref,
                       cbw_ref, wout_hbm, o_ref, abuf, bbuf, win_ref, wout_ref, *, weight_layer):
    tm = x_ref.shape[1]

    @pl.when((pl.program_id(0) == 0) & (pl.program_id(1) == 0))
    def _():
        _fetch_weight_bf16(win_hbm, weight_layer, win_ref)
        _fetch_weight_bf16(wout_hbm, weight_layer, wout_ref)
        abuf[...] = jnp.zeros(abuf.shape, jnp.float32)
        bbuf[...] = jnp.zeros(bbuf.shape, jnp.float32)

    sequence_start = pl.program_id(1) == 0
    _slab_carry(abuf, A_HIST, tm, sequence_start)
    _slab_carry(bbuf, B_HIST, tm, sequence_start)

    x = x_ref[0]
    h = (_rms_scale(x) * g_ref[...]).astype(jnp.bfloat16)

    def proj(lo, width):
        return jnp.dot(h, win_ref[:, lo:lo + width], preferred_element_type=jnp.float32)

    piece = 2 * LANES
    n_pieces = D_A // piece
    chunks_per_piece = piece // LANES
    order = [(name, q) for q in range(n_pieces) for name in ("a_val", "a_gate")]
    order += [(name, q) for q in range(n_pieces) for name in ("b_x", "b_c")]
    order += [("b_b", q) for q in range(n_pieces)]
    first_col = {"a_val": 0, "a_gate": D_A, "b_x": 2 * D_A, "b_c": 2 * D_A + D_B,
                 "b_b": 2 * D_A + 2 * D_B}
    res, glu_chunks = {}, {}
    for idx, (name, q) in enumerate(order):
        r = proj(first_col[name] + q * piece, piece)
        res[name, q] = r
        if name == "a_gate":
            glu = res["a_val", q] * jax.nn.sigmoid(r)
            for half in range(chunks_per_piece):
                glu_chunks[q * chunks_per_piece + half] = glu[:, half * LANES:(half + 1) * LANES]
        if name == "b_c":
            cx = r * res["b_x", q]
            for half in range(chunks_per_piece):
                _slab_store_chunk(bbuf, B_HIST, q * chunks_per_piece + half,
                                  cx[:, half * LANES:(half + 1) * LANES])
        c, between = divmod(idx - RELEASE_LAG, RELEASE_STRIDE)
        if between == 0 and 0 <= c < D_A // LANES:
            released = jnp.where(r[:, :LANES] > jnp.inf, 0.0, glu_chunks[c])
            _slab_store_chunk(abuf, A_HIST, c, released)
    b_b = jnp.concatenate([res["b_b", q] for q in range(n_pieces)], axis=1)

    acc = _causal_depthwise_conv(abuf, A_HIST, caw_ref, CONV_A_WIDTH, tm, cab_ref)
    mu = jnp.mean(acc, axis=-1, keepdims=True)
    xc = acc - mu
    var = jnp.mean(xc * xc, axis=-1, keepdims=True)
    y = xc * lax.rsqrt(var + LN_EPS) * lng_ref[...] + lnb_ref[...]
    a_out = y * jax.nn.sigmoid(y)

    bo = b_b * _causal_depthwise_conv(bbuf, B_HIST, cbw_ref, CONV_B_WIDTH, tm)

    cat = jnp.concatenate([a_out, bo], axis=-1).astype(jnp.bfloat16)
    o_ref[0] = x + jnp.dot(cat, wout_ref[...], preferred_element_type=jnp.float32)


def _even_mixer(x, g_all, layer, w_in, conv_a_w, conv_a_b, ln_g, ln_b, conv_b_w, w_out, i):
    bsz, seq, d = x.shape
    tm = TM_EVEN
    return pl.pallas_call(
        functools.partial(_even_mixer_kernel, weight_layer=i),
        name="even_mixer",
        grid=(bsz, seq // tm),
        in_specs=[
            pl.BlockSpec((1, tm, d), lambda b, j: (b, j, 0)),
            _stacked_spec(g_all, layer),
            pl.BlockSpec(memory_space=pl.ANY),
            _stacked_spec(conv_a_w, i),
            _stacked_spec(conv_a_b, i),
            _stacked_spec(ln_g, i),
            _stacked_spec(ln_b, i),
            _stacked_spec(conv_b_w, i),
            pl.BlockSpec(memory_space=pl.ANY),
        ],
        out_specs=pl.BlockSpec((1, tm, d), lambda b, j: (b, j, 0)),
        out_shape=jax.ShapeDtypeStruct(x.shape, x.dtype),
        scratch_shapes=[
            pltpu.VMEM((D_A // LANES, _slab_rows(A_HIST, tm), LANES), jnp.float32),
            pltpu.VMEM((D_B // LANES, _slab_rows(B_HIST, tm), LANES), jnp.float32),
            pltpu.VMEM((d, D_IN_EVEN), jnp.bfloat16),
            pltpu.VMEM((D_A + D_B, d), jnp.bfloat16),
        ],
        compiler_params=pltpu.CompilerParams(
            dimension_semantics=("arbitrary", "arbitrary"),
            vmem_limit_bytes=VMEM_LIMIT_BYTES),
    )(x, g_all, w_in, conv_a_w, conv_a_b, ln_g, ln_b, conv_b_w, w_out)


def _pool_mixer_kernel(x_ref, g_ref, pw_hbm, ps_ref, o_ref, hbuf, pw_ref, *, weight_layer):
    tm = x_ref.shape[1]
    j = pl.program_id(1)

    @pl.when((pl.program_id(0) == 0) & (j == 0))
    def _():
        _fetch_weight_bf16(pw_hbm, weight_layer, pw_ref)
        hbuf[...] = jnp.zeros(hbuf.shape, jnp.float32)

    _slab_carry(hbuf, POOL_HIST, tm, j == 0)

    x = x_ref[0]
    h = _rms_scale(x) * g_ref[...]
    _slab_store(hbuf, POOL_HIST, h)
    pos = (j * tm + 1 + lax.broadcasted_iota(jnp.int32, (tm, 1), 0)).astype(jnp.float32)

    chunks_per_group = POOL_GROUP_DIM // LANES
    for grp, win in enumerate(POOL_WINDOWS):
        lo = grp * POOL_GROUP_DIM
        cols = slice(lo, lo + POOL_GROUP_DIM)
        sums = []
        for c in range(grp * chunks_per_group, (grp + 1) * chunks_per_group):
            s = h[:, c * LANES:(c + 1) * LANES]
            for lag in range(1, win):
                s = s + _slab_window(hbuf, c, POOL_HIST - lag, tm)
            sums.append(s)
        pooled = jnp.concatenate(sums, axis=1) / jnp.minimum(pos, float(win)) - h[:, cols]
        mixed = jnp.dot(pooled.astype(jnp.bfloat16), pw_ref[lo:lo + POOL_GROUP_DIM, :],
                        preferred_element_type=jnp.float32)
        o_ref[0, :, cols] = x[:, cols] + mixed * ps_ref[:, cols]


def _pool_mixer(x, g_all, layer, pool_w_rows, pool_scale_all, i):
    bsz, seq, d = x.shape
    tm = TM_POOL
    return pl.pallas_call(
        functools.partial(_pool_mixer_kernel, weight_layer=i),
        name="pool_mixer",
        grid=(bsz, seq // tm),
        in_specs=[
            pl.BlockSpec((1, tm, d), lambda b, j: (b, j, 0)),
            _stacked_spec(g_all, layer),
            pl.BlockSpec(memory_space=pl.ANY),
            _stacked_spec(pool_scale_all, i),
        ],
        out_specs=pl.BlockSpec((1, tm, d), lambda b, j: (b, j, 0)),
        out_shape=jax.ShapeDtypeStruct(x.shape, x.dtype),
        scratch_shapes=[
            pltpu.VMEM((d // LANES, _slab_rows(POOL_HIST, tm), LANES), jnp.float32),
            pltpu.VMEM(pool_w_rows.shape[1:], jnp.bfloat16),
        ],
        compiler_params=pltpu.CompilerParams(
            dimension_semantics=("arbitrary", "arbitrary"),
            vmem_limit_bytes=VMEM_LIMIT_BYTES),
    )(x, g_all, pool_w_rows, pool_scale_all)


def _mlp_kernel(x_hbm, g_ref, w1_ref, w2_ref, fg_ref, o_ref, xbuf, h_ref, sem, *,
                apply_final_norm):
    i, f = pl.program_id(0), pl.program_id(1)
    tm = o_ref.shape[0]

    def x_copy(tile):
        return pltpu.make_async_copy(x_hbm.at[pl.ds(tile * tm, tm), :], xbuf, sem)

    @pl.when(f == 0)
    def _():
        @pl.when(i == 0)
        def _():
            x_copy(0).start()

        x_copy(i).wait()
        x = xbuf[...]
        h_ref[...] = (_rms_scale(x) * g_ref[...]).astype(jnp.bfloat16)
        o_ref[...] = x

    @pl.when((f == 1) & (i + 1 < pl.num_programs(0)))
    def _():
        x_copy(i + 1).start()

    w1 = w1_ref[...].astype(jnp.bfloat16)
    z = jnp.dot(h_ref[...], w1, preferred_element_type=jnp.float32)
    z = jnp.square(jnp.maximum(z, 0.0)).astype(jnp.bfloat16)
    w2 = w2_ref[...].astype(jnp.bfloat16)
    o_ref[...] += jnp.dot(z, w2, preferred_element_type=jnp.float32)

    if apply_final_norm:
        @pl.when(f == pl.num_programs(1) - 1)
        def _():
            o_ref[...] = _rms_scale(o_ref[...]) * fg_ref[...]


def _mlp(x2d, g_all, w1_all, w2_all, layer, final_g, apply_final_norm):
    n, d = x2d.shape
    tm, tf = TM_MLP, TF_MLP
    assert n % tm == 0 and D_FF % tf == 0 and D_FF // tf >= 2
    return pl.pallas_call(
        functools.partial(_mlp_kernel, apply_final_norm=apply_final_norm),
        name="mlp",
        grid=(n // tm, D_FF // tf),
        in_specs=[
            pl.BlockSpec(memory_space=pl.ANY),
            _stacked_spec(g_all, layer),
            pl.BlockSpec((None, d, tf), lambda i, f: (layer, 0, f)),
            pl.BlockSpec((None, tf, d), lambda i, f: (layer, f, 0)),
            _const_spec((1, d)),
        ],
        out_specs=pl.BlockSpec((tm, d), lambda i, f: (i, 0)),
        out_shape=jax.ShapeDtypeStruct(x2d.shape, x2d.dtype),
        scratch_shapes=[
            pltpu.VMEM((tm, d), jnp.float32),
            pltpu.VMEM((tm, d), jnp.bfloat16),
            pltpu.SemaphoreType.DMA(()),
        ],
        compiler_params=pltpu.CompilerParams(
            dimension_semantics=("arbitrary", "arbitrary"),
            vmem_limit_bytes=VMEM_LIMIT_BYTES),
    )(x2d, g_all, w1_all, w2_all, final_g.reshape(1, -1))


def kernel(x, norm_mix_g, norm_mlp_g, w_in_even, conv_a_w, conv_a_b, ln_a_g, ln_a_b,
           conv_b_w, w_out_even, pool_w, pool_scale, mlp_w1, mlp_w2, final_g):
    bsz, seq, d = x.shape
    depth = norm_mix_g.shape[0]
    n_odd, n_groups, gdim, _ = pool_w.shape
    pool_w_rows = pool_w.reshape(n_odd, n_groups * gdim, gdim)
    rows = lambda stacked: stacked.reshape(stacked.shape[0], 1, stacked.shape[1])
    mix_g, mlp_g = rows(norm_mix_g), rows(norm_mlp_g)
    conv_a_b, ln_a_g, ln_a_b, pool_scale = rows(conv_a_b), rows(ln_a_g), rows(ln_a_b), rows(pool_scale)
    for layer in range(depth):
        i = layer // 2
        if layer % 2 == 0:
            x = _even_mixer(x, mix_g, layer, w_in_even, conv_a_w, conv_a_b, ln_a_g, ln_a_b,
                            conv_b_w, w_out_even, i)
        else:
            x = _pool_mixer(x, mix_g, layer, pool_w_rows, pool_scale, i)
        x = _mlp(x.reshape(bsz * seq, d), mlp_g, mlp_w1, mlp_w2, layer, final_g,
                 apply_final_norm=(layer == depth - 1)).reshape(bsz, seq, d)
    return x
```

```python
import functools

import jax
import jax.numpy as jnp
from jax import lax
from jax.experimental import pallas as pl
from jax.experimental.pallas import tpu as pltpu

D_MODEL = 2048
D_A = D_MODEL // 2
D_B = D_MODEL // 2
D_IN_EVEN = 2 * D_A + 3 * D_B
CONV_A_WIDTH = 31
CONV_B_WIDTH = 3
POOL_WINDOWS = (2, 4, 8, 16)
POOL_GROUP_DIM = D_MODEL // len(POOL_WINDOWS)
D_FF = 4 * D_MODEL
RMS_EPS = 1e-6
LN_EPS = 1e-5

VMEM_LIMIT_BYTES = 56 * 1024 * 1024

A_HIST = 32
B_HIST = 8
POOL_HIST = 16

LANES = 128
ROW_PITCH = 2
CONV_ROWS = 128
RELEASE_LAG = 2
RELEASE_STRIDE = 2
WEIGHT_STAGE_SLOTS = 8
WEIGHT_CHUNK_BYTES = 3 << 19

TM_EVEN = 256
TM_POOL = 512
TM_MLP = 1024
TF_MLP = 512


def _rms_scale(x):
    ms = jnp.mean(x * x, axis=-1, keepdims=True)
    return x * lax.rsqrt(ms + RMS_EPS)


def _const_spec(shape):
    zeros = (0,) * len(shape)
    return pl.BlockSpec(shape, lambda *_: zeros, pipeline_mode=pl.Buffered(1))


def _stacked_spec(stacked, index):
    _, rows, cols = stacked.shape
    return pl.BlockSpec((None, rows, cols), lambda *_: (index, 0, 0), pipeline_mode=pl.Buffered(1))


def _slab_rows(hist, tm):
    return ROW_PITCH * (hist + tm)


def _slab_window(buf, c, token, rows):
    return buf[c, pl.ds(ROW_PITCH * token, rows, stride=ROW_PITCH), :]


def _slab_store(buf, hist, value):
    for c in range(value.shape[1] // LANES):
        _slab_store_chunk(buf, hist, c, value[:, c * LANES:(c + 1) * LANES])


def _slab_store_chunk(buf, hist, c, value):
    buf[c, pl.ds(ROW_PITCH * hist, value.shape[0], stride=ROW_PITCH), :] = value


def _slab_carry(buf, hist, tm, reset):
    for c in range(buf.shape[0]):
        tail = _slab_window(buf, c, tm, hist)
        buf[c, pl.ds(0, hist, stride=ROW_PITCH), :] = jnp.where(reset, 0.0, tail)


def _causal_depthwise_conv(buf, hist, w_ref, width, tm, bias_ref=None):
    outs = []
    for c in range(buf.shape[0]):
        lanes = slice(c * LANES, (c + 1) * LANES)
        parts = []
        for r0 in range(0, tm, CONV_ROWS):
            acc = None
            for k in range(width):
                window = _slab_window(buf, c, hist - (width - 1) + k + r0, CONV_ROWS)
                term = w_ref[k:k + 1, lanes] * window
                acc = term if acc is None else acc + term
            if bias_ref is not None:
                acc = acc + bias_ref[:, lanes]
            parts.append(acc)
        outs.append(jnp.concatenate(parts, axis=0))
    return jnp.concatenate(outs, axis=1)


def _fetch_weight_bf16(w_hbm, layer, w_vmem):
    k, n = w_vmem.shape
    rows_per_chunk = max(r for r in range(16, k + 1, 16)
                         if k % r == 0 and 4 * r * n <= WEIGHT_CHUNK_BYTES)
    n_chunks = k // rows_per_chunk
    n_slots = min(WEIGHT_STAGE_SLOTS, n_chunks)
    ahead = n_slots - 1
    assert ahead >= 1

    def body(stage, sem):
        def chunk_copy(c):
            slot = c % n_slots
            src = w_hbm.at[layer, pl.ds(c * rows_per_chunk, rows_per_chunk), :]
            return pltpu.make_async_copy(src, stage.at[slot], sem.at[slot])

        for c in range(ahead):
            chunk_copy(c).start()

        @pl.loop(0, n_chunks)
        def _(c):
            chunk_copy(c).wait()

            @pl.when(c + ahead < n_chunks)
            def _():
                chunk_copy(c + ahead).start()

            r0 = pl.multiple_of(c * rows_per_chunk, rows_per_chunk)
            w_vmem[pl.ds(r0, rows_per_chunk), :] = stage[c % n_slots].astype(jnp.bfloat16)

    pl.run_scoped(body, pltpu.VMEM((n_slots, rows_per_chunk, n), jnp.float32),
                  pltpu.SemaphoreType.DMA((n_slots,)))


def _even_mixer_kernel(x_ref, g_ref, win_hbm, caw_ref, cab_ref, lng_ref, lnb_ref,
                       cbw_ref, wout_hbm, o_ref, abuf, bbuf, win_ref, wout_ref, *, weight_layer):
    tm = x_ref.shape[1]

    @pl.when((pl.program_id(0) == 0) & (pl.program_id(1) == 0))
    def _():
        _fetch_weight_bf16(win_hbm, weight_layer, win_ref)
        _fetch_weight_bf16(wout_hbm, weight_layer, wout_ref)
        abuf[...] = jnp.zeros(abuf.shape, jnp.float32)
        bbuf[...] = jnp.zeros(bbuf.shape, jnp.float32)

    sequence_start = pl.program_id(1) == 0
    _slab_carry(abuf, A_HIST, tm, sequence_start)
    _slab_carry(bbuf, B_HIST, tm, sequence_start)

    x = x_ref[0]
    h = (_rms_scale(x) * g_ref[...]).astype(jnp.bfloat16)

    def proj(lo, width):
        return jnp.dot(h, win_ref[:, lo:lo + width], preferred_element_type=jnp.float32)

    piece = 2 * LANES
    n_pieces = D_A // piece
    chunks_per_piece = piece // LANES
    order = [(name, q) for q in range(n_pieces) for name in ("a_val", "a_gate")]
    order += [(name, q) for q in range(n_pieces) for name in ("b_x", "b_c")]
    order += [("b_b", q) for q in range(n_pieces)]
    first_col = {"a_val": 0, "a_gate": D_A, "b_x": 2 * D_A, "b_c": 2 * D_A + D_B,
                 "b_b": 2 * D_A + 2 * D_B}
    res, glu_chunks = {}, {}
    for idx, (name, q) in enumerate(order):
        r = proj(first_col[name] + q * piece, piece)
        res[name, q] = r
        if name == "a_gate":
            glu = res["a_val", q] * jax.nn.sigmoid(r)
            for half in range(chunks_per_piece):
                glu_chunks[q * chunks_per_piece + half] = glu[:, half * LANES:(half + 1) * LANES]
        if name == "b_c":
            cx = r * res["b_x", q]
            for half in range(chunks_per_piece):
                _slab_store_chunk(bbuf, B_HIST, q * chunks_per_piece + half,
                                  cx[:, half * LANES:(half + 1) * LANES])
        c, between = divmod(idx - RELEASE_LAG, RELEASE_STRIDE)
        if between == 0 and 0 <= c < D_A // LANES:
            released = jnp.where(r[:, :LANES] > jnp.inf, 0.0, glu_chunks[c])
            _slab_store_chunk(abuf, A_HIST, c, released)
    b_b = jnp.concatenate([res["b_b", q] for q in range(n_pieces)], axis=1)

    acc = _causal_depthwise_conv(abuf, A_HIST, caw_ref, CONV_A_WIDTH, tm, cab_ref)
    mu = jnp.mean(acc, axis=-1, keepdims=True)
    xc = acc - mu
    var = jnp.mean(xc * xc, axis=-1, keepdims=True)
    y = xc * lax.rsqrt(var + LN_EPS) * lng_ref[...] + lnb_ref[...]
    a_out = y * jax.nn.sigmoid(y)

    bo = b_b * _causal_depthwise_conv(bbuf, B_HIST, cbw_ref, CONV_B_WIDTH, tm)

    cat = jnp.concatenate([a_out, bo], axis=-1).astype(jnp.bfloat16)
    o_ref[0] = x + jnp.dot(cat, wout_ref[...], preferred_element_type=jnp.float32)


def _even_mixer(x, g_all, layer, w_in, conv_a_w, conv_a_b, ln_g, ln_b, conv_b_w, w_out, i):
    bsz, seq, d = x.shape
    tm = TM_EVEN
    return pl.pallas_call(
        functools.partial(_even_mixer_kernel, weight_layer=i),
        name="even_mixer",
        grid=(bsz, seq // tm),
        in_specs=[
            pl.BlockSpec((1, tm, d), lambda b, j: (b, j, 0)),
            _stacked_spec(g_all, layer),
            pl.BlockSpec(memory_space=pl.ANY),
            _stacked_spec(conv_a_w, i),
            _stacked_spec(conv_a_b, i),
            _stacked_spec(ln_g, i),
            _stacked_spec(ln_b, i),
            _stacked_spec(conv_b_w, i),
            pl.BlockSpec(memory_space=pl.ANY),
        ],
        out_specs=pl.BlockSpec((1, tm, d), lambda b, j: (b, j, 0)),
        out_shape=jax.ShapeDtypeStruct(x.shape, x.dtype),
        scratch_shapes=[
            pltpu.VMEM((D_A // LANES, _slab_rows(A_HIST, tm), LANES), jnp.float32),
            pltpu.VMEM((D_B // LANES, _slab_rows(B_HIST, tm), LANES), jnp.float32),
            pltpu.VMEM((d, D_IN_EVEN), jnp.bfloat16),
            pltpu.VMEM((D_A + D_B, d), jnp.bfloat16),
        ],
        compiler_params=pltpu.CompilerParams(
            dimension_semantics=("arbitrary", "arbitrary"),
            vmem_limit_bytes=VMEM_LIMIT_BYTES),
    )(x, g_all, w_in, conv_a_w, conv_a_b, ln_g, ln_b, conv_b_w, w_out)


def _pool_mixer_kernel(x_ref, g_ref, pw_hbm, ps_ref, o_ref, hbuf, pw_ref, *, weight_layer):
    tm = x_ref.shape[1]
    j = pl.program_id(1)

    @pl.when((pl.program_id(0) == 0) & (j == 0))
    def _():
        _fetch_weight_bf16(pw_hbm, weight_layer, pw_ref)
        hbuf[...] = jnp.zeros(hbuf.shape, jnp.float32)

    _slab_carry(hbuf, POOL_HIST, tm, j == 0)

    x = x_ref[0]
    h = _rms_scale(x) * g_ref[...]
    _slab_store(hbuf, POOL_HIST, h)
    pos = (j * tm + 1 + lax.broadcasted_iota(jnp.int32, (tm, 1), 0)).astype(jnp.float32)

    chunks_per_group = POOL_GROUP_DIM // LANES
    for grp, win in enumerate(POOL_WINDOWS):
        lo = grp * POOL_GROUP_DIM
        cols = slice(lo, lo + POOL_GROUP_DIM)
        sums = []
        for c in range(grp * chunks_per_group, (grp + 1) * chunks_per_group):
            s = h[:, c * LANES:(c + 1) * LANES]
            for lag in range(1, win):
                s = s + _slab_window(hbuf, c, POOL_HIST - lag, tm)
            sums.append(s)
        pooled = jnp.concatenate(sums, axis=1) / jnp.minimum(pos, float(win)) - h[:, cols]
        mixed = jnp.dot(pooled.astype(jnp.bfloat16), pw_ref[lo:lo + POOL_GROUP_DIM, :],
                        preferred_element_type=jnp.float32)
        o_ref[0, :, cols] = x[:, cols] + mixed * ps_ref[:, cols]


def _pool_mixer(x, g_all, layer, pool_w_rows, pool_scale_all, i):
    bsz, seq, d = x.shape
    tm = TM_POOL
    return pl.pallas_call(
        functools.partial(_pool_mixer_kernel, weight_layer=i),
        name="pool_mixer",
        grid=(bsz, seq // tm),
        in_specs=[
            pl.BlockSpec((1, tm, d), lambda b, j: (b, j, 0)),
            _stacked_spec(g_all, layer),
            pl.BlockSpec(memory_space=pl.ANY),
            _stacked_spec(pool_scale_all, i),
        ],
        out_specs=pl.BlockSpec((1, tm, d), lambda b, j: (b, j, 0)),
        out_shape=jax.ShapeDtypeStruct(x.shape, x.dtype),
        scratch_shapes=[
            pltpu.VMEM((d // LANES, _slab_rows(POOL_HIST, tm), LANES), jnp.float32),
            pltpu.VMEM(pool_w_rows.shape[1:], jnp.bfloat16),
        ],
        compiler_params=pltpu.CompilerParams(
            dimension_semantics=("arbitrary", "arbitrary"),
            vmem_limit_bytes=VMEM_LIMIT_BYTES),
    )(x, g_all, pool_w_rows, pool_scale_all)


def _mlp_kernel(x_hbm, g_ref, w1_ref, w2_ref, fg_ref, o_ref, xbuf, h_ref, sem, *,
                apply_final_norm):
    i, f = pl.program_id(0), pl.program_id(1)
    tm = o_ref.shape[0]

    def x_copy(tile):
        return pltpu.make_async_copy(x_hbm.at[pl.ds(tile * tm, tm), :], xbuf, sem)

    @pl.when(f == 0)
    def _():
        @pl.when(i == 0)
        def _():
            x_copy(0).start()

        x_copy(i).wait()
        x = xbuf[...]
        h_ref[...] = (_rms_scale(x) * g_ref[...]).astype(jnp.bfloat16)
        o_ref[...] = x

    @pl.when((f == 1) & (i + 1 < pl.num_programs(0)))
    def _():
        x_copy(i + 1).start()

    w1 = w1_ref[...].astype(jnp.bfloat16)
    z = jnp.dot(h_ref[...], w1, preferred_element_type=jnp.float32)
    z = jnp.square(jnp.maximum(z, 0.0)).astype(jnp.bfloat16)
    w2 = w2_ref[...].astype(jnp.bfloat16)
    o_ref[...] += jnp.dot(z, w2, preferred_element_type=jnp.float32)

    if apply_final_norm:
        @pl.when(f == pl.num_programs(1) - 1)
        def _():
            o_ref[...] = _rms_scale(o_ref[...]) * fg_ref[...]


def _mlp(x2d, g_all, w1_all, w2_all, layer, final_g, apply_final_norm):
    n, d = x2d.shape
    tm, tf = TM_MLP, TF_MLP
    assert n % tm == 0 and D_FF % tf == 0 and D_FF // tf >= 2
    return pl.pallas_call(
        functools.partial(_mlp_kernel, apply_final_norm=apply_final_norm),
        name="mlp",
        grid=(n // tm, D_FF // tf),
        in_specs=[
            pl.BlockSpec(memory_space=pl.ANY),
            _stacked_spec(g_all, layer),
            pl.BlockSpec((None, d, tf), lambda i, f: (layer, 0, f)),
            pl.BlockSpec((None, tf, d), lambda i, f: (layer, f, 0)),
            _const_spec((1, d)),
        ],
        out_specs=pl.BlockSpec((tm, d), lambda i, f: (i, 0)),
        out_shape=jax.ShapeDtypeStruct(x2d.shape, x2d.dtype),
        scratch_shapes=[
            pltpu.VMEM((tm, d), jnp.float32),
            pltpu.VMEM((tm, d), jnp.bfloat16),
            pltpu.SemaphoreType.DMA(()),
        ],
        compiler_params=pltpu.CompilerParams(
            dimension_semantics=("arbitrary", "arbitrary"),
            vmem_limit_bytes=VMEM_LIMIT_BYTES),
    )(x2d, g_all, w1_all, w2_all, final_g.reshape(1, -1))


def kernel(x, norm_mix_g, norm_mlp_g, w_in_even, conv_a_w, conv_a_b, ln_a_g, ln_a_b,
           conv_b_w, w_out_even, pool_w, pool_scale, mlp_w1, mlp_w2, final_g):
    bsz, seq, d = x.shape
    depth = norm_mix_g.shape[0]
    n_odd, n_groups, gdim, _ = pool_w.shape
    pool_w_rows = pool_w.reshape(n_odd, n_groups * gdim, gdim)
    rows = lambda stacked: stacked.reshape(stacked.shape[0], 1, stacked.shape[1])
    mix_g, mlp_g = rows(norm_mix_g), rows(norm_mlp_g)
    conv_a_b, ln_a_g, ln_a_b, pool_scale = rows(conv_a_b), rows(ln_a_g), rows(ln_a_b), rows(pool_scale)
    for layer in range(depth):
        i = layer // 2
        if layer % 2 == 0:
            x = _even_mixer(x, mix_g, layer, w_in_even, conv_a_w, conv_a_b, ln_a_g, ln_a_b,
                            conv_b_w, w_out_even, i)
        else:
            x = _pool_mixer(x, mix_g, layer, pool_w_rows, pool_scale, i)
        x = _mlp(x.reshape(bsz * seq, d), mlp_g, mlp_w1, mlp_w2, layer, final_g,
                 apply_final_norm=(layer == depth - 1)).reshape(bsz, seq, d)
    return x
```

```python
import functools

import jax
import jax.numpy as jnp
from jax import lax
from jax.experimental import pallas as pl
from jax.experimental.pallas import tpu as pltpu

D_MODEL = 2048
D_A = D_MODEL // 2
D_B = D_MODEL // 2
D_IN_EVEN = 2 * D_A + 3 * D_B
CONV_A_WIDTH = 31
CONV_B_WIDTH = 3
POOL_WINDOWS = (2, 4, 8, 16)
POOL_GROUP_DIM = D_MODEL // len(POOL_WINDOWS)
D_FF = 4 * D_MODEL
RMS_EPS = 1e-6
LN_EPS = 1e-5

VMEM_LIMIT_BYTES = 56 * 1024 * 1024

A_HIST = 32
B_HIST = 8
POOL_HIST = 16

LANES = 128
ROW_PITCH = 2
CONV_ROWS = 128
RELEASE_LAG = 2
RELEASE_STRIDE = 2
WEIGHT_STAGE_SLOTS = 4
WEIGHT_CHUNK_BYTES = 3 << 19

TM_EVEN = 256
TM_POOL = 512
TM_MLP = 1024
TF_MLP = 512


def _rms_scale(x):
    ms = jnp.mean(x * x, axis=-1, keepdims=True)
    return x * lax.rsqrt(ms + RMS_EPS)


def _const_spec(shape):
    zeros = (0,) * len(shape)
    return pl.BlockSpec(shape, lambda *_: zeros, pipeline_mode=pl.Buffered(1))


def _stacked_spec(stacked, index):
    _, rows, cols = stacked.shape
    return pl.BlockSpec((None, rows, cols), lambda *_: (index, 0, 0), pipeline_mode=pl.Buffered(1))


def _slab_rows(hist, tm):
    return ROW_PITCH * (hist + tm)


def _slab_window(buf, c, token, rows):
    return buf[c, pl.ds(ROW_PITCH * token, rows, stride=ROW_PITCH), :]


def _slab_store(buf, hist, value):
    for c in range(value.shape[1] // LANES):
        _slab_store_chunk(buf, hist, c, value[:, c * LANES:(c + 1) * LANES])


def _slab_store_chunk(buf, hist, c, value):
    buf[c, pl.ds(ROW_PITCH * hist, value.shape[0], stride=ROW_PITCH), :] = value


def _slab_carry(buf, hist, tm, reset):
    for c in range(buf.shape[0]):
        tail = _slab_window(buf, c, tm, hist)
        buf[c, pl.ds(0, hist, stride=ROW_PITCH), :] = jnp.where(reset, 0.0, tail)


def _causal_depthwise_conv(buf, hist, w_ref, width, tm, bias_ref=None):
    outs = []
    for c in range(buf.shape[0]):
        lanes = slice(c * LANES, (c + 1) * LANES)
        parts = []
        for r0 in range(0, tm, CONV_ROWS):
            acc = None
            for k in range(width):
                window = _slab_window(buf, c, hist - (width - 1) + k + r0, CONV_ROWS)
                term = w_ref[k:k + 1, lanes] * window
                acc = term if acc is None else acc + term
            if bias_ref is not None:
                acc = acc + bias_ref[:, lanes]
            parts.append(acc)
        outs.append(jnp.concatenate(parts, axis=0))
    return jnp.concatenate(outs, axis=1)


def _fetch_weight_bf16(w_hbm, layer, w_vmem):
    k, n = w_vmem.shape
    rows_per_chunk = max(r for r in range(16, k + 1, 16)
                         if k % r == 0 and 4 * r * n <= WEIGHT_CHUNK_BYTES)
    n_chunks = k // rows_per_chunk
    n_slots = min(WEIGHT_STAGE_SLOTS, n_chunks)
    ahead = n_slots - 1
    assert ahead >= 1

    def body(stage, sem):
        def chunk_copy(c):
            slot = c % n_slots
            src = w_hbm.at[layer, pl.ds(c * rows_per_chunk, rows_per_chunk), :]
            return pltpu.make_async_copy(src, stage.at[slot], sem.at[slot])

        for c in range(ahead):
            chunk_copy(c).start()

        @pl.loop(0, n_chunks)
        def _(c):
            chunk_copy(c).wait()

            @pl.when(c + ahead < n_chunks)
            def _():
                chunk_copy(c + ahead).start()

            r0 = pl.multiple_of(c * rows_per_chunk, rows_per_chunk)
            w_vmem[pl.ds(r0, rows_per_chunk), :] = stage[c % n_slots].astype(jnp.bfloat16)

    pl.run_scoped(body, pltpu.VMEM((n_slots, rows_per_chunk, n), jnp.float32),
                  pltpu.SemaphoreType.DMA((n_slots,)))


def _even_mixer_kernel(x_ref, g_ref, win_hbm, caw_ref, cab_ref, lng_ref, lnb_ref,
                       cbw_ref, wout_hbm, o_ref, abuf, bbuf, win_ref, wout_ref, *, weight_layer):
    tm = x_ref.shape[1]

    @pl.when((pl.program_id(0) == 0) & (pl.program_id(1) == 0))
    def _():
        _fetch_weight_bf16(win_hbm, weight_layer, win_ref)
        _fetch_weight_bf16(wout_hbm, weight_layer, wout_ref)
        abuf[...] = jnp.zeros(abuf.shape, jnp.float32)
        bbuf[...] = jnp.zeros(bbuf.shape, jnp.float32)

    sequence_start = pl.program_id(1) == 0
    _slab_carry(abuf, A_HIST, tm, sequence_start)
    _slab_carry(bbuf, B_HIST, tm, sequence_start)

    x = x_ref[0]
    h = (_rms_scale(x) * g_ref[...]).astype(jnp.bfloat16)

    def proj(lo, width):
        return jnp.dot(h, win_ref[:, lo:lo + width], preferred_element_type=jnp.float32)

    piece = 2 * LANES
    n_pieces = D_A // piece
    chunks_per_piece = piece // LANES
    order = [(name, q) for q in range(n_pieces) for name in ("a_val", "a_gate")]
    order += [(name, q) for q in range(n_pieces) for name in ("b_x", "b_c")]
    order += [("b_b", q) for q in range(n_pieces)]
    first_col = {"a_val": 0, "a_gate": D_A, "b_x": 2 * D_A, "b_c": 2 * D_A + D_B,
                 "b_b": 2 * D_A + 2 * D_B}
    res, glu_chunks = {}, {}
    for idx, (name, q) in enumerate(order):
        r = proj(first_col[name] + q * piece, piece)
        res[name, q] = r
        if name == "a_gate":
            glu = res["a_val", q] * jax.nn.sigmoid(r)
            for half in range(chunks_per_piece):
                glu_chunks[q * chunks_per_piece + half] = glu[:, half * LANES:(half + 1) * LANES]
        if name == "b_c":
            cx = r * res["b_x", q]
            for half in range(chunks_per_piece):
                _slab_store_chunk(bbuf, B_HIST, q * chunks_per_piece + half,
                                  cx[:, half * LANES:(half + 1) * LANES])
        c, between = divmod(idx - RELEASE_LAG, RELEASE_STRIDE)
        if between == 0 and 0 <= c < D_A // LANES:
            released = jnp.where(r[:, :LANES] > jnp.inf, 0.0, glu_chunks[c])
            _slab_store_chunk(abuf, A_HIST, c, released)
    b_b = jnp.concatenate([res["b_b", q] for q in range(n_pieces)], axis=1)

    acc = _causal_depthwise_conv(abuf, A_HIST, caw_ref, CONV_A_WIDTH, tm, cab_ref)
    mu = jnp.mean(acc, axis=-1, keepdims=True)
    xc = acc - mu
    var = jnp.mean(xc * xc, axis=-1, keepdims=True)
    y = xc * lax.rsqrt(var + LN_EPS) * lng_ref[...] + lnb_ref[...]
    a_out = y * jax.nn.sigmoid(y)

    bo = b_b * _causal_depthwise_conv(bbuf, B_HIST, cbw_ref, CONV_B_WIDTH, tm)

    cat = jnp.concatenate([a_out, bo], axis=-1).astype(jnp.bfloat16)
    o_ref[0] = x + jnp.dot(cat, wout_ref[...], preferred_element_type=jnp.float32)


def _even_mixer(x, g_all, layer, w_in, conv_a_w, conv_a_b, ln_g, ln_b, conv_b_w, w_out, i):
    bsz, seq, d = x.shape
    tm = TM_EVEN
    return pl.pallas_call(
        functools.partial(_even_mixer_kernel, weight_layer=i),
        name="even_mixer",
        grid=(bsz, seq // tm),
        in_specs=[
            pl.BlockSpec((1, tm, d), lambda b, j: (b, j, 0)),
            _stacked_spec(g_all, layer),
            pl.BlockSpec(memory_space=pl.ANY),
            _stacked_spec(conv_a_w, i),
            _stacked_spec(conv_a_b, i),
            _stacked_spec(ln_g, i),
            _stacked_spec(ln_b, i),
            _stacked_spec(conv_b_w, i),
            pl.BlockSpec(memory_space=pl.ANY),
        ],
        out_specs=pl.BlockSpec((1, tm, d), lambda b, j: (b, j, 0)),
        out_shape=jax.ShapeDtypeStruct(x.shape, x.dtype),
        scratch_shapes=[
            pltpu.VMEM((D_A // LANES, _slab_rows(A_HIST, tm), LANES), jnp.float32),
            pltpu.VMEM((D_B // LANES, _slab_rows(B_HIST, tm), LANES), jnp.float32),
            pltpu.VMEM((d, D_IN_EVEN), jnp.bfloat16),
            pltpu.VMEM((D_A + D_B, d), jnp.bfloat16),
        ],
        compiler_params=pltpu.CompilerParams(
            dimension_semantics=("arbitrary", "arbitrary"),
            vmem_limit_bytes=VMEM_LIMIT_BYTES),
    )(x, g_all, w_in, conv_a_w, conv_a_b, ln_g, ln_b, conv_b_w, w_out)


def _pool_mixer_kernel(x_ref, g_ref, pw_hbm, ps_ref, o_ref, hbuf, lvl1, lvl2, lvl3, pw_ref, *,
                       weight_layer):
    tm = x_ref.shape[1]
    j = pl.program_id(1)
    chunks_per_group = POOL_GROUP_DIM // LANES
    levels = (hbuf, lvl1, lvl2, lvl3)
    assert POOL_WINDOWS == tuple(2 ** (g + 1) for g in range(len(levels)))

    @pl.when((pl.program_id(0) == 0) & (j == 0))
    def _():
        _fetch_weight_bf16(pw_hbm, weight_layer, pw_ref)
        for buf in levels:
            buf[...] = jnp.zeros(buf.shape, jnp.float32)

    for buf in levels:
        _slab_carry(buf, POOL_HIST, tm, j == 0)

    x = x_ref[0]
    h = _rms_scale(x) * g_ref[...]
    _slab_store(hbuf, POOL_HIST, h)
    pos = (j * tm + 1 + lax.broadcasted_iota(jnp.int32, (tm, 1), 0)).astype(jnp.float32)

    for grp, win in enumerate(POOL_WINDOWS):
        lo = grp * POOL_GROUP_DIM
        cols = slice(lo, lo + POOL_GROUP_DIM)
        n_levels = grp + 1
        sums = []
        for c in range(grp * chunks_per_group, (grp + 1) * chunks_per_group):
            s = h[:, c * LANES:(c + 1) * LANES]
            for k in range(1, n_levels + 1):
                src_chunk = c - (k - 1) * chunks_per_group
                s = s + _slab_window(levels[k - 1], src_chunk, POOL_HIST - 2 ** (k - 1), tm)
                if k < n_levels:
                    _slab_store_chunk(levels[k], POOL_HIST, c - k * chunks_per_group, s)
            sums.append(s)
        pooled = jnp.concatenate(sums, axis=1) / jnp.minimum(pos, float(win)) - h[:, cols]
        mixed = jnp.dot(pooled.astype(jnp.bfloat16), pw_ref[lo:lo + POOL_GROUP_DIM, :],
                        preferred_element_type=jnp.float32)
        o_ref[0, :, cols] = x[:, cols] + mixed * ps_ref[:, cols]


def _pool_mixer(x, g_all, layer, pool_w_rows, pool_scale_all, i):
    bsz, seq, d = x.shape
    tm = TM_POOL
    return pl.pallas_call(
        functools.partial(_pool_mixer_kernel, weight_layer=i),
        name="pool_mixer",
        grid=(bsz, seq // tm),
        in_specs=[
            pl.BlockSpec((1, tm, d), lambda b, j: (b, j, 0)),
            _stacked_spec(g_all, layer),
            pl.BlockSpec(memory_space=pl.ANY),
            _stacked_spec(pool_scale_all, i),
        ],
        out_specs=pl.BlockSpec((1, tm, d), lambda b, j: (b, j, 0)),
        out_shape=jax.ShapeDtypeStruct(x.shape, x.dtype),
        scratch_shapes=[
            *[pltpu.VMEM(((len(POOL_WINDOWS) - k) * POOL_GROUP_DIM // LANES,
                          _slab_rows(POOL_HIST, tm), LANES), jnp.float32)
              for k in range(len(POOL_WINDOWS))],
            pltpu.VMEM(pool_w_rows.shape[1:], jnp.bfloat16),
        ],
        compiler_params=pltpu.CompilerParams(
            dimension_semantics=("arbitrary", "arbitrary"),
            vmem_limit_bytes=VMEM_LIMIT_BYTES),
    )(x, g_all, pool_w_rows, pool_scale_all)


def _mlp_kernel(x_hbm, g_ref, w1_ref, w2_ref, fg_ref, o_ref, xbuf, h_ref, sem, *,
                apply_final_norm):
    i, f = pl.program_id(0), pl.program_id(1)
    tm = o_ref.shape[0]

    def x_copy(tile):
        return pltpu.make_async_copy(x_hbm.at[pl.ds(tile * tm, tm), :], xbuf, sem)

    @pl.when(f == 0)
    def _():
        @pl.when(i == 0)
        def _():
            x_copy(0).start()

        x_copy(i).wait()
        x = xbuf[...]
        h_ref[...] = (_rms_scale(x) * g_ref[...]).astype(jnp.bfloat16)
        o_ref[...] = x

    @pl.when((f == 1) & (i + 1 < pl.num_programs(0)))
    def _():
        x_copy(i + 1).start()

    w1 = w1_ref[...].astype(jnp.bfloat16)
    z = jnp.dot(h_ref[...], w1, preferred_element_type=jnp.float32)
    z = jnp.square(jnp.maximum(z, 0.0)).astype(jnp.bfloat16)
    w2 = w2_ref[...].astype(jnp.bfloat16)
    o_ref[...] += jnp.dot(z, w2, preferred_element_type=jnp.float32)

    if apply_final_norm:
        @pl.when(f == pl.num_programs(1) - 1)
        def _():
            o_ref[...] = _rms_scale(o_ref[...]) * fg_ref[...]


def _mlp(x2d, g_all, w1_all, w2_all, layer, final_g, apply_final_norm):
    n, d = x2d.shape
    tm, tf = TM_MLP, TF_MLP
    assert n % tm == 0 and D_FF % tf == 0 and D_FF // tf >= 2
    return pl.pallas_call(
        functools.partial(_mlp_kernel, apply_final_norm=apply_final_norm),
        name="mlp",
        grid=(n // tm, D_FF // tf),
        in_specs=[
            pl.BlockSpec(memory_space=pl.ANY),
            _stacked_spec(g_all, layer),
            pl.BlockSpec((None, d, tf), lambda i, f: (layer, 0, f)),
            pl.BlockSpec((None, tf, d), lambda i, f: (layer, f, 0)),
            _const_spec((1, d)),
        ],
        out_specs=pl.BlockSpec((tm, d), lambda i, f: (i, 0)),
        out_shape=jax.ShapeDtypeStruct(x2d.shape, x2d.dtype),
        scratch_shapes=[
            pltpu.VMEM((tm, d), jnp.float32),
            pltpu.VMEM((tm, d), jnp.bfloat16),
            pltpu.SemaphoreType.DMA(()),
        ],
        compiler_params=pltpu.CompilerParams(
            dimension_semantics=("arbitrary", "arbitrary"),
            vmem_limit_bytes=VMEM_LIMIT_BYTES),
    )(x2d, g_all, w1_all, w2_all, final_g.reshape(1, -1))


def kernel(x, norm_mix_g, norm_mlp_g, w_in_even, conv_a_w, conv_a_b, ln_a_g, ln_a_b,
           conv_b_w, w_out_even, pool_w, pool_scale, mlp_w1, mlp_w2, final_g):
    bsz, seq, d = x.shape
    depth = norm_mix_g.shape[0]
    n_odd, n_groups, gdim, _ = pool_w.shape
    pool_w_rows = pool_w.reshape(n_odd, n_groups * gdim, gdim)
    rows = lambda stacked: stacked.reshape(stacked.shape[0], 1, stacked.shape[1])
    mix_g, mlp_g = rows(norm_mix_g), rows(norm_mlp_g)
    conv_a_b, ln_a_g, ln_a_b, pool_scale = rows(conv_a_b), rows(ln_a_g), rows(ln_a_b), rows(pool_scale)
    for layer in range(depth):
        i = layer // 2
        if layer % 2 == 0:
            x = _even_mixer(x, mix_g, layer, w_in_even, conv_a_w, conv_a_b, ln_a_g, ln_a_b,
                            conv_b_w, w_out_even, i)
        else:
            x = _pool_mixer(x, mix_g, layer, pool_w_rows, pool_scale, i)
        x = _mlp(x.reshape(bsz * seq, d), mlp_g, mlp_w1, mlp_w2, layer, final_g,
                 apply_final_norm=(layer == depth - 1)).reshape(bsz, seq, d)
    return x
```

```python
import functools

import jax
import jax.numpy as jnp
from jax import lax
from jax.experimental import pallas as pl
from jax.experimental.pallas import tpu as pltpu

D_MODEL = 2048
D_A = D_MODEL // 2
D_B = D_MODEL // 2
D_IN_EVEN = 2 * D_A + 3 * D_B
CONV_A_WIDTH = 31
CONV_B_WIDTH = 3
POOL_WINDOWS = (2, 4, 8, 16)
POOL_GROUP_DIM = D_MODEL // len(POOL_WINDOWS)
D_FF = 4 * D_MODEL
RMS_EPS = 1e-6
LN_EPS = 1e-5

VMEM_LIMIT_BYTES = 56 * 1024 * 1024

A_HIST = 32
B_HIST = 8
POOL_HIST = 16

LANES = 128
ROW_PITCH = 2
CONV_ROWS = 128
RELEASE_LAG = 2
RELEASE_STRIDE = 2
WEIGHT_STAGE_SLOTS = 8
WEIGHT_CHUNK_BYTES = 3 << 19

TM_EVEN = 256
TM_POOL = 512
TM_MLP = 1024
TF_MLP = 512


def _rms_scale(x):
    ms = jnp.mean(x * x, axis=-1, keepdims=True)
    return x * lax.rsqrt(ms + RMS_EPS)


def _const_spec(shape):
    zeros = (0,) * len(shape)
    return pl.BlockSpec(shape, lambda *_: zeros, pipeline_mode=pl.Buffered(1))


def _stacked_spec(stacked, index):
    _, rows, cols = stacked.shape
    return pl.BlockSpec((None, rows, cols), lambda *_: (index, 0, 0), pipeline_mode=pl.Buffered(1))


def _slab_rows(hist, tm):
    return ROW_PITCH * (hist + tm)


def _slab_window(buf, c, token, rows):
    return buf[c, pl.ds(ROW_PITCH * token, rows, stride=ROW_PITCH), :]


def _slab_store(buf, hist, value):
    for c in range(value.shape[1] // LANES):
        _slab_store_chunk(buf, hist, c, value[:, c * LANES:(c + 1) * LANES])


def _slab_store_chunk(buf, hist, c, value):
    buf[c, pl.ds(ROW_PITCH * hist, value.shape[0], stride=ROW_PITCH), :] = value


def _slab_carry(buf, hist, tm, reset):
    for c in range(buf.shape[0]):
        tail = _slab_window(buf, c, tm, hist)
        buf[c, pl.ds(0, hist, stride=ROW_PITCH), :] = jnp.where(reset, 0.0, tail)


def _causal_depthwise_conv(buf, hist, w_ref, width, tm, bias_ref=None):
    outs = []
    for c in range(buf.shape[0]):
        lanes = slice(c * LANES, (c + 1) * LANES)
        parts = []
        for r0 in range(0, tm, CONV_ROWS):
            acc = None
            for k in range(width):
                window = _slab_window(buf, c, hist - (width - 1) + k + r0, CONV_ROWS)
                term = w_ref[k:k + 1, lanes] * window
                acc = term if acc is None else acc + term
            if bias_ref is not None:
                acc = acc + bias_ref[:, lanes]
            parts.append(acc)
        outs.append(jnp.concatenate(parts, axis=0))
    return jnp.concatenate(outs, axis=1)


def _fetch_weight_bf16(w_hbm, layer, w_vmem):
    k, n = w_vmem.shape
    rows_per_chunk = max(r for r in range(16, k + 1, 16)
                         if k % r == 0 and 4 * r * n <= WEIGHT_CHUNK_BYTES)
    n_chunks = k // rows_per_chunk
    n_slots = min(WEIGHT_STAGE_SLOTS, n_chunks)
    ahead = n_slots - 1
    assert ahead >= 1

    def body(stage, sem):
        def chunk_copy(c):
            slot = c % n_slots
            src = w_hbm.at[layer, pl.ds(c * rows_per_chunk, rows_per_chunk), :]
            return pltpu.make_async_copy(src, stage.at[slot], sem.at[slot])

        for c in range(ahead):
            chunk_copy(c).start()

        @pl.loop(0, n_chunks)
        def _(c):
            chunk_copy(c).wait()

            @pl.when(c + ahead < n_chunks)
            def _():
                chunk_copy(c + ahead).start()

            r0 = pl.multiple_of(c * rows_per_chunk, rows_per_chunk)
            w_vmem[pl.ds(r0, rows_per_chunk), :] = stage[c % n_slots].astype(jnp.bfloat16)

    pl.run_scoped(body, pltpu.VMEM((n_slots, rows_per_chunk, n), jnp.float32),
                  pltpu.SemaphoreType.DMA((n_slots,)))


def _even_mixer_kernel(x_ref, g_ref, win_hbm, caw_ref, cab_ref, lng_ref, lnb_ref,
                       cbw_ref, wout_hbm, o_ref, abuf, bbuf, win_ref, wout_ref, *, weight_layer):
    tm = x_ref.shape[1]

    @pl.when((pl.program_id(0) == 0) & (pl.program_id(1) == 0))
    def _():
        _fetch_weight_bf16(win_hbm, weight_layer, win_ref)
        _fetch_weight_bf16(wout_hbm, weight_layer, wout_ref)
        abuf[...] = jnp.zeros(abuf.shape, jnp.float32)
        bbuf[...] = jnp.zeros(bbuf.shape, jnp.float32)

    sequence_start = pl.program_id(1) == 0
    _slab_carry(abuf, A_HIST, tm, sequence_start)
    _slab_carry(bbuf, B_HIST, tm, sequence_start)

    x = x_ref[0]
    h = (_rms_scale(x) * g_ref[...]).astype(jnp.bfloat16)

    def proj(lo, width):
        return jnp.dot(h, win_ref[:, lo:lo + width], preferred_element_type=jnp.float32)

    piece = 2 * LANES
    n_pieces = D_A // piece
    chunks_per_piece = piece // LANES
    order = [(name, q) for q in range(n_pieces) for name in ("a_val", "a_gate")]
    order += [(name, q) for q in range(n_pieces) for name in ("b_x", "b_c")]
    order += [("b_b", q) for q in range(n_pieces)]
    first_col = {"a_val": 0, "a_gate": D_A, "b_x": 2 * D_A, "b_c": 2 * D_A + D_B,
                 "b_b": 2 * D_A + 2 * D_B}
    res, glu_chunks = {}, {}
    for idx, (name, q) in enumerate(order):
        r = proj(first_col[name] + q * piece, piece)
        res[name, q] = r
        if name == "a_gate":
            glu = res["a_val", q] * jax.nn.sigmoid(r)
            for half in range(chunks_per_piece):
                glu_chunks[q * chunks_per_piece + half] = glu[:, half * LANES:(half + 1) * LANES]
        if name == "b_c":
            cx = r * res["b_x", q]
            for half in range(chunks_per_piece):
                _slab_store_chunk(bbuf, B_HIST, q * chunks_per_piece + half,
                                  cx[:, half * LANES:(half + 1) * LANES])
        c, between = divmod(idx - RELEASE_LAG, RELEASE_STRIDE)
        if between == 0 and 0 <= c < D_A // LANES:
            released = jnp.where(r[:, :LANES] > jnp.inf, 0.0, glu_chunks[c])
            _slab_store_chunk(abuf, A_HIST, c, released)
    b_b = jnp.concatenate([res["b_b", q] for q in range(n_pieces)], axis=1)

    acc = _causal_depthwise_conv(abuf, A_HIST, caw_ref, CONV_A_WIDTH, tm, cab_ref)
    mu = jnp.mean(acc, axis=-1, keepdims=True)
    xc = acc - mu
    var = jnp.mean(xc * xc, axis=-1, keepdims=True)
    y = xc * lax.rsqrt(var + LN_EPS) * lng_ref[...] + lnb_ref[...]
    a_out = y * jax.nn.sigmoid(y)

    bo = b_b * _causal_depthwise_conv(bbuf, B_HIST, cbw_ref, CONV_B_WIDTH, tm)

    cat = jnp.concatenate([a_out, bo], axis=-1).astype(jnp.bfloat16)
    o_ref[0] = x + jnp.dot(cat, wout_ref[...], preferred_element_type=jnp.float32)


def _even_mixer(x, g_all, layer, w_in, conv_a_w, conv_a_b, ln_g, ln_b, conv_b_w, w_out, i):
    bsz, seq, d = x.shape
    tm = TM_EVEN
    return pl.pallas_call(
        functools.partial(_even_mixer_kernel, weight_layer=i),
        name="even_mixer",
        grid=(bsz, seq // tm),
        in_specs=[
            pl.BlockSpec((1, tm, d), lambda b, j: (b, j, 0)),
            _stacked_spec(g_all, layer),
            pl.BlockSpec(memory_space=pl.ANY),
            _stacked_spec(conv_a_w, i),
            _stacked_spec(conv_a_b, i),
            _stacked_spec(ln_g, i),
            _stacked_spec(ln_b, i),
            _stacked_spec(conv_b_w, i),
            pl.BlockSpec(memory_space=pl.ANY),
        ],
        out_specs=pl.BlockSpec((1, tm, d), lambda b, j: (b, j, 0)),
        out_shape=jax.ShapeDtypeStruct(x.shape, x.dtype),
        scratch_shapes=[
            pltpu.VMEM((D_A // LANES, _slab_rows(A_HIST, tm), LANES), jnp.float32),
            pltpu.VMEM((D_B // LANES, _slab_rows(B_HIST, tm), LANES), jnp.float32),
            pltpu.VMEM((d, D_IN_EVEN), jnp.bfloat16),
            pltpu.VMEM((D_A + D_B, d), jnp.bfloat16),
        ],
        compiler_params=pltpu.CompilerParams(
            dimension_semantics=("arbitrary", "arbitrary"),
            vmem_limit_bytes=VMEM_LIMIT_BYTES),
    )(x, g_all, w_in, conv_a_w, conv_a_b, ln_g, ln_b, conv_b_w, w_out)


def _pool_mixer_kernel(x_ref, g_ref, pw_hbm, ps_ref, o_ref, hbuf, lvl1, lvl2, lvl3, pw_ref, *,
                       weight_layer):
    tm = x_ref.shape[1]
    j = pl.program_id(1)
    chunks_per_group = POOL_GROUP_DIM // LANES
    levels = (hbuf, lvl1, lvl2, lvl3)
    assert POOL_WINDOWS == tuple(2 ** (g + 1) for g in range(len(levels)))

    @pl.when((pl.program_id(0) == 0) & (j == 0))
    def _():
        _fetch_weight_bf16(pw_hbm, weight_layer, pw_ref)
        for buf in levels:
            buf[...] = jnp.zeros(buf.shape, jnp.float32)

    for buf in levels:
        _slab_carry(buf, POOL_HIST, tm, j == 0)

    x = x_ref[0]
    h = _rms_scale(x) * g_ref[...]
    _slab_store(hbuf, POOL_HIST, h)
    pos = (j * tm + 1 + lax.broadcasted_iota(jnp.int32, (tm, 1), 0)).astype(jnp.float32)

    for grp, win in enumerate(POOL_WINDOWS):
        lo = grp * POOL_GROUP_DIM
        cols = slice(lo, lo + POOL_GROUP_DIM)
        n_levels = grp + 1
        sums = []
        for c in range(grp * chunks_per_group, (grp + 1) * chunks_per_group):
            s = h[:, c * LANES:(c + 1) * LANES]
            for k in range(1, n_levels + 1):
                src_chunk = c - (k - 1) * chunks_per_group
                s = s + _slab_window(levels[k - 1], src_chunk, POOL_HIST - 2 ** (k - 1), tm)
                if k < n_levels:
                    _slab_store_chunk(levels[k], POOL_HIST, c - k * chunks_per_group, s)
            sums.append(s)
        pooled = jnp.concatenate(sums, axis=1) / jnp.minimum(pos, float(win)) - h[:, cols]
        mixed = jnp.dot(pooled.astype(jnp.bfloat16), pw_ref[lo:lo + POOL_GROUP_DIM, :],
                        preferred_element_type=jnp.float32)
        o_ref[0, :, cols] = x[:, cols] + mixed * ps_ref[:, cols]


def _pool_mixer(x, g_all, layer, pool_w_rows, pool_scale_all, i):
    bsz, seq, d = x.shape
    tm = TM_POOL
    return pl.pallas_call(
        functools.partial(_pool_mixer_kernel, weight_layer=i),
        name="pool_mixer",
        grid=(bsz, seq // tm),
        in_specs=[
            pl.BlockSpec((1, tm, d), lambda b, j: (b, j, 0)),
            _stacked_spec(g_all, layer),
            pl.BlockSpec(memory_space=pl.ANY),
            _stacked_spec(pool_scale_all, i),
        ],
        out_specs=pl.BlockSpec((1, tm, d), lambda b, j: (b, j, 0)),
        out_shape=jax.ShapeDtypeStruct(x.shape, x.dtype),
        scratch_shapes=[
            *[pltpu.VMEM(((len(POOL_WINDOWS) - k) * POOL_GROUP_DIM // LANES,
                          _slab_rows(POOL_HIST, tm), LANES), jnp.float32)
              for k in range(len(POOL_WINDOWS))],
            pltpu.VMEM(pool_w_rows.shape[1:], jnp.bfloat16),
        ],
        compiler_params=pltpu.CompilerParams(
            dimension_semantics=("arbitrary", "arbitrary"),
            vmem_limit_bytes=VMEM_LIMIT_BYTES),
    )(x, g_all, pool_w_rows, pool_scale_all)


def _mlp_kernel(x_hbm, g_ref, w1_ref, w2_ref, fg_ref, o_ref, xbuf, h_ref, sem, *,
                apply_final_norm):
    i, f = pl.program_id(0), pl.program_id(1)
    tm = o_ref.shape[0]

    def x_copy(tile):
        return pltpu.make_async_copy(x_hbm.at[pl.ds(tile * tm, tm), :], xbuf, sem)

    @pl.when(f == 0)
    def _():
        @pl.when(i == 0)
        def _():
            x_copy(0).start()

        x_copy(i).wait()
        x = xbuf[...]
        h_ref[...] = (_rms_scale(x) * g_ref[...]).astype(jnp.bfloat16)
        o_ref[...] = x

    @pl.when((f == 1) & (i + 1 < pl.num_programs(0)))
    def _():
        x_copy(i + 1).start()

    w1 = w1_ref[...].astype(jnp.bfloat16)
    z = jnp.dot(h_ref[...], w1, preferred_element_type=jnp.float32)
    z = jnp.square(jnp.maximum(z, 0.0)).astype(jnp.bfloat16)
    w2 = w2_ref[...].astype(jnp.bfloat16)
    o_ref[...] += jnp.dot(z, w2, preferred_element_type=jnp.float32)

    if apply_final_norm:
        @pl.when(f == pl.num_programs(1) - 1)
        def _():
            o_ref[...] = _rms_scale(o_ref[...]) * fg_ref[...]


def _mlp(x2d, g_all, w1_all, w2_all, layer, final_g, apply_final_norm):
    n, d = x2d.shape
    tm, tf = TM_MLP, TF_MLP
    assert n % tm == 0 and D_FF % tf == 0 and D_FF // tf >= 2
    return pl.pallas_call(
        functools.partial(_mlp_kernel, apply_final_norm=apply_final_norm),
        name="mlp",
        grid=(n // tm, D_FF // tf),
        in_specs=[
            pl.BlockSpec(memory_space=pl.ANY),
            _stacked_spec(g_all, layer),
            pl.BlockSpec((None, d, tf), lambda i, f: (layer, 0, f)),
            pl.BlockSpec((None, tf, d), lambda i, f: (layer, f, 0)),
            _const_spec((1, d)),
        ],
        out_specs=pl.BlockSpec((tm, d), lambda i, f: (i, 0)),
        out_shape=jax.ShapeDtypeStruct(x2d.shape, x2d.dtype),
        scratch_shapes=[
            pltpu.VMEM((tm, d), jnp.float32),
            pltpu.VMEM((tm, d), jnp.bfloat16),
            pltpu.SemaphoreType.DMA(()),
        ],
        compiler_params=pltpu.CompilerParams(
            dimension_semantics=("arbitrary", "arbitrary"),
            vmem_limit_bytes=VMEM_LIMIT_BYTES),
    )(x2d, g_all, w1_all, w2_all, final_g.reshape(1, -1))


def kernel(x, norm_mix_g, norm_mlp_g, w_in_even, conv_a_w, conv_a_b, ln_a_g, ln_a_b,
           conv_b_w, w_out_even, pool_w, pool_scale, mlp_w1, mlp_w2, final_g):
    bsz, seq, d = x.shape
    depth = norm_mix_g.shape[0]
    n_odd, n_groups, gdim, _ = pool_w.shape
    pool_w_rows = pool_w.reshape(n_odd, n_groups * gdim, gdim)
    rows = lambda stacked: stacked.reshape(stacked.shape[0], 1, stacked.shape[1])
    mix_g, mlp_g = rows(norm_mix_g), rows(norm_mlp_g)
    conv_a_b, ln_a_g, ln_a_b, pool_scale = rows(conv_a_b), rows(ln_a_g), rows(ln_a_b), rows(pool_scale)
    for layer in range(depth):
        i = layer // 2
        if layer % 2 == 0:
            x = _even_mixer(x, mix_g, layer, w_in_even, conv_a_w, conv_a_b, ln_a_g, ln_a_b,
                            conv_b_w, w_out_even, i)
        else:
            x = _pool_mixer(x, mix_g, layer, pool_w_rows, pool_scale, i)
        x = _mlp(x.reshape(bsz * seq, d), mlp_g, mlp_w1, mlp_w2, layer, final_g,
                 apply_final_norm=(layer == depth - 1)).reshape(bsz, seq, d)
    return x
```

```python
import functools

import jax
import jax.numpy as jnp
from jax import lax
from jax.experimental import pallas as pl
from jax.experimental.pallas import tpu as pltpu

D_MODEL = 2048
D_A = D_MODEL // 2
D_B = D_MODEL // 2
D_IN_EVEN = 2 * D_A + 3 * D_B
CONV_A_WIDTH = 31
CONV_B_WIDTH = 3
POOL_WINDOWS = (2, 4, 8, 16)
POOL_GROUP_DIM = D_MODEL // len(POOL_WINDOWS)
D_FF = 4 * D_MODEL
RMS_EPS = 1e-6
LN_EPS = 1e-5

VMEM_LIMIT_BYTES = 56 * 1024 * 1024

A_HIST = 32
B_HIST = 8
POOL_HIST = 16

LANES = 128
ROW_PITCH = 2
CONV_ROWS = 128
RELEASE_LAG = 2
RELEASE_STRIDE = 2
WEIGHT_STAGE_SLOTS = 8
WEIGHT_CHUNK_BYTES = 3 << 19

TM_EVEN = 256
TM_POOL = 512
TM_MLP = 1024
TF_MLP = 512


def _rms_scale(x):
    ms = jnp.mean(x * x, axis=-1, keepdims=True)
    return x * lax.rsqrt(ms + RMS_EPS)


def _const_spec(shape):
    zeros = (0,) * len(shape)
    return pl.BlockSpec(shape, lambda *_: zeros, pipeline_mode=pl.Buffered(1))


def _stacked_spec(stacked, index):
    _, rows, cols = stacked.shape
    return pl.BlockSpec((None, rows, cols), lambda *_: (index, 0, 0), pipeline_mode=pl.Buffered(1))


def _row(ref, index):
    return ref.at[pl.ds(index, 1), :]


def _slab_rows(hist, tm):
    return ROW_PITCH * (hist + tm)


def _slab_window(buf, c, token, rows):
    return buf[c, pl.ds(ROW_PITCH * token, rows, stride=ROW_PITCH), :]


def _slab_store(buf, hist, value):
    for c in range(value.shape[1] // LANES):
        _slab_store_chunk(buf, hist, c, value[:, c * LANES:(c + 1) * LANES])


def _slab_store_chunk(buf, hist, c, value):
    buf[c, pl.ds(ROW_PITCH * hist, value.shape[0], stride=ROW_PITCH), :] = value


def _slab_carry(buf, hist, tm, reset):
    for c in range(buf.shape[0]):
        tail = _slab_window(buf, c, tm, hist)
        buf[c, pl.ds(0, hist, stride=ROW_PITCH), :] = jnp.where(reset, 0.0, tail)


def _causal_depthwise_conv(buf, hist, w_ref, width, tm, bias_ref=None):
    outs = []
    for c in range(buf.shape[0]):
        lanes = slice(c * LANES, (c + 1) * LANES)
        parts = []
        for r0 in range(0, tm, CONV_ROWS):
            acc = None
            for k in range(width):
                window = _slab_window(buf, c, hist - (width - 1) + k + r0, CONV_ROWS)
                term = w_ref[k:k + 1, lanes] * window
                acc = term if acc is None else acc + term
            if bias_ref is not None:
                acc = acc + bias_ref[:, lanes]
            parts.append(acc)
        outs.append(jnp.concatenate(parts, axis=0))
    return jnp.concatenate(outs, axis=1)


def _fetch_weight_bf16(w_hbm, layer, w_vmem):
    k, n = w_vmem.shape
    rows_per_chunk = max(r for r in range(16, k + 1, 16)
                         if k % r == 0 and 4 * r * n <= WEIGHT_CHUNK_BYTES)
    n_chunks = k // rows_per_chunk
    n_slots = min(WEIGHT_STAGE_SLOTS, n_chunks)
    ahead = n_slots - 1
    assert ahead >= 1

    def body(stage, sem):
        def chunk_copy(c):
            slot = c % n_slots
            src = w_hbm.at[layer, pl.ds(c * rows_per_chunk, rows_per_chunk), :]
            return pltpu.make_async_copy(src, stage.at[slot], sem.at[slot])

        for c in range(ahead):
            chunk_copy(c).start()

        @pl.loop(0, n_chunks)
        def _(c):
            chunk_copy(c).wait()

            @pl.when(c + ahead < n_chunks)
            def _():
                chunk_copy(c + ahead).start()

            r0 = pl.multiple_of(c * rows_per_chunk, rows_per_chunk)
            w_vmem[pl.ds(r0, rows_per_chunk), :] = stage[c % n_slots].astype(jnp.bfloat16)

    pl.run_scoped(body, pltpu.VMEM((n_slots, rows_per_chunk, n), jnp.float32),
                  pltpu.SemaphoreType.DMA((n_slots,)))


def _even_mixer_kernel(x_ref, g_ref, win_hbm, caw_ref, cab_ref, lng_ref, lnb_ref,
                       cbw_ref, wout_hbm, o_ref, abuf, bbuf, win_ref, wout_ref, *, weight_layer,
                       norm_layer):
    tm = x_ref.shape[1]
    g_ref = _row(g_ref, norm_layer)
    cab_ref, lng_ref, lnb_ref = (_row(r, weight_layer) for r in (cab_ref, lng_ref, lnb_ref))

    @pl.when((pl.program_id(0) == 0) & (pl.program_id(1) == 0))
    def _():
        _fetch_weight_bf16(win_hbm, weight_layer, win_ref)
        _fetch_weight_bf16(wout_hbm, weight_layer, wout_ref)
        abuf[...] = jnp.zeros(abuf.shape, jnp.float32)
        bbuf[...] = jnp.zeros(bbuf.shape, jnp.float32)

    sequence_start = pl.program_id(1) == 0
    _slab_carry(abuf, A_HIST, tm, sequence_start)
    _slab_carry(bbuf, B_HIST, tm, sequence_start)

    x = x_ref[0]
    h = (_rms_scale(x) * g_ref[...]).astype(jnp.bfloat16)

    def proj(lo, width):
        return jnp.dot(h, win_ref[:, lo:lo + width], preferred_element_type=jnp.float32)

    piece = 2 * LANES
    n_pieces = D_A // piece
    chunks_per_piece = piece // LANES
    order = [(name, q) for q in range(n_pieces) for name in ("a_val", "a_gate")]
    order += [(name, q) for q in range(n_pieces) for name in ("b_x", "b_c")]
    order += [("b_b", q) for q in range(n_pieces)]
    first_col = {"a_val": 0, "a_gate": D_A, "b_x": 2 * D_A, "b_c": 2 * D_A + D_B,
                 "b_b": 2 * D_A + 2 * D_B}
    res, glu_chunks = {}, {}
    for idx, (name, q) in enumerate(order):
        r = proj(first_col[name] + q * piece, piece)
        res[name, q] = r
        if name == "a_gate":
            glu = res["a_val", q] * jax.nn.sigmoid(r)
            for half in range(chunks_per_piece):
                glu_chunks[q * chunks_per_piece + half] = glu[:, half * LANES:(half + 1) * LANES]
        if name == "b_c":
            cx = r * res["b_x", q]
            for half in range(chunks_per_piece):
                _slab_store_chunk(bbuf, B_HIST, q * chunks_per_piece + half,
                                  cx[:, half * LANES:(half + 1) * LANES])
        c, between = divmod(idx - RELEASE_LAG, RELEASE_STRIDE)
        if between == 0 and 0 <= c < D_A // LANES:
            released = jnp.where(r[:, :LANES] > jnp.inf, 0.0, glu_chunks[c])
            _slab_store_chunk(abuf, A_HIST, c, released)
    b_b = jnp.concatenate([res["b_b", q] for q in range(n_pieces)], axis=1)

    acc = _causal_depthwise_conv(abuf, A_HIST, caw_ref, CONV_A_WIDTH, tm, cab_ref)
    mu = jnp.mean(acc, axis=-1, keepdims=True)
    xc = acc - mu
    var = jnp.mean(xc * xc, axis=-1, keepdims=True)
    y = xc * lax.rsqrt(var + LN_EPS) * lng_ref[...] + lnb_ref[...]
    a_out = y * jax.nn.sigmoid(y)

    bo = b_b * _causal_depthwise_conv(bbuf, B_HIST, cbw_ref, CONV_B_WIDTH, tm)

    cat = jnp.concatenate([a_out, bo], axis=-1).astype(jnp.bfloat16)
    o_ref[0] = x + jnp.dot(cat, wout_ref[...], preferred_element_type=jnp.float32)


def _even_mixer(x, g_all, layer, w_in, conv_a_w, conv_a_b, ln_g, ln_b, conv_b_w, w_out, i):
    bsz, seq, d = x.shape
    tm = TM_EVEN
    return pl.pallas_call(
        functools.partial(_even_mixer_kernel, weight_layer=i, norm_layer=layer),
        name="even_mixer",
        grid=(bsz, seq // tm),
        in_specs=[
            pl.BlockSpec((1, tm, d), lambda b, j: (b, j, 0)),
            _const_spec(g_all.shape),
            pl.BlockSpec(memory_space=pl.ANY),
            _stacked_spec(conv_a_w, i),
            _const_spec(conv_a_b.shape),
            _const_spec(ln_g.shape),
            _const_spec(ln_b.shape),
            _stacked_spec(conv_b_w, i),
            pl.BlockSpec(memory_space=pl.ANY),
        ],
        out_specs=pl.BlockSpec((1, tm, d), lambda b, j: (b, j, 0)),
        out_shape=jax.ShapeDtypeStruct(x.shape, x.dtype),
        scratch_shapes=[
            pltpu.VMEM((D_A // LANES, _slab_rows(A_HIST, tm), LANES), jnp.float32),
            pltpu.VMEM((D_B // LANES, _slab_rows(B_HIST, tm), LANES), jnp.float32),
            pltpu.VMEM((d, D_IN_EVEN), jnp.bfloat16),
            pltpu.VMEM((D_A + D_B, d), jnp.bfloat16),
        ],
        compiler_params=pltpu.CompilerParams(
            dimension_semantics=("arbitrary", "arbitrary"),
            vmem_limit_bytes=VMEM_LIMIT_BYTES),
    )(x, g_all, w_in, conv_a_w, conv_a_b, ln_g, ln_b, conv_b_w, w_out)


def _pool_mixer_kernel(x_ref, g_ref, pw_hbm, ps_ref, o_ref, hbuf, lvl1, lvl2, lvl3, pw_ref, *,
                       weight_layer, norm_layer):
    tm = x_ref.shape[1]
    j = pl.program_id(1)
    g_ref, ps_ref = _row(g_ref, norm_layer), _row(ps_ref, weight_layer)
    chunks_per_group = POOL_GROUP_DIM // LANES
    levels = (hbuf, lvl1, lvl2, lvl3)
    assert POOL_WINDOWS == tuple(2 ** (g + 1) for g in range(len(levels)))

    @pl.when((pl.program_id(0) == 0) & (j == 0))
    def _():
        _fetch_weight_bf16(pw_hbm, weight_layer, pw_ref)
        for buf in levels:
            buf[...] = jnp.zeros(buf.shape, jnp.float32)

    for buf in levels:
        _slab_carry(buf, POOL_HIST, tm, j == 0)

    x = x_ref[0]
    h = _rms_scale(x) * g_ref[...]
    _slab_store(hbuf, POOL_HIST, h)
    pos = (j * tm + 1 + lax.broadcasted_iota(jnp.int32, (tm, 1), 0)).astype(jnp.float32)

    for grp, win in enumerate(POOL_WINDOWS):
        lo = grp * POOL_GROUP_DIM
        cols = slice(lo, lo + POOL_GROUP_DIM)
        n_levels = grp + 1
        sums = []
        for c in range(grp * chunks_per_group, (grp + 1) * chunks_per_group):
            s = h[:, c * LANES:(c + 1) * LANES]
            for k in range(1, n_levels + 1):
                src_chunk = c - (k - 1) * chunks_per_group
                s = s + _slab_window(levels[k - 1], src_chunk, POOL_HIST - 2 ** (k - 1), tm)
                if k < n_levels:
                    _slab_store_chunk(levels[k], POOL_HIST, c - k * chunks_per_group, s)
            sums.append(s)
        pooled = jnp.concatenate(sums, axis=1) / jnp.minimum(pos, float(win)) - h[:, cols]
        mixed = jnp.dot(pooled.astype(jnp.bfloat16), pw_ref[lo:lo + POOL_GROUP_DIM, :],
                        preferred_element_type=jnp.float32)
        o_ref[0, :, cols] = x[:, cols] + mixed * ps_ref[:, cols]


def _pool_mixer(x, g_all, layer, pool_w_rows, pool_scale_all, i):
    bsz, seq, d = x.shape
    tm = TM_POOL
    return pl.pallas_call(
        functools.partial(_pool_mixer_kernel, weight_layer=i, norm_layer=layer),
        name="pool_mixer",
        grid=(bsz, seq // tm),
        in_specs=[
            pl.BlockSpec((1, tm, d), lambda b, j: (b, j, 0)),
            _const_spec(g_all.shape),
            pl.BlockSpec(memory_space=pl.ANY),
            _const_spec(pool_scale_all.shape),
        ],
        out_specs=pl.BlockSpec((1, tm, d), lambda b, j: (b, j, 0)),
        out_shape=jax.ShapeDtypeStruct(x.shape, x.dtype),
        scratch_shapes=[
            *[pltpu.VMEM(((len(POOL_WINDOWS) - k) * POOL_GROUP_DIM // LANES,
                          _slab_rows(POOL_HIST, tm), LANES), jnp.float32)
              for k in range(len(POOL_WINDOWS))],
            pltpu.VMEM(pool_w_rows.shape[1:], jnp.bfloat16),
        ],
        compiler_params=pltpu.CompilerParams(
            dimension_semantics=("arbitrary", "arbitrary"),
            vmem_limit_bytes=VMEM_LIMIT_BYTES),
    )(x, g_all, pool_w_rows, pool_scale_all)


def _mlp_kernel(x_hbm, g_ref, w1_ref, w2_ref, fg_ref, o_ref, xbuf, h_ref, sem, *,
                norm_layer, apply_final_norm):
    i, f = pl.program_id(0), pl.program_id(1)
    tm = o_ref.shape[0]
    g_ref = _row(g_ref, norm_layer)

    def x_copy(tile):
        return pltpu.make_async_copy(x_hbm.at[pl.ds(tile * tm, tm), :], xbuf, sem)

    @pl.when(f == 0)
    def _():
        @pl.when(i == 0)
        def _():
            x_copy(0).start()

        x_copy(i).wait()
        x = xbuf[...]
        h_ref[...] = (_rms_scale(x) * g_ref[...]).astype(jnp.bfloat16)
        o_ref[...] = x

    @pl.when((f == 1) & (i + 1 < pl.num_programs(0)))
    def _():
        x_copy(i + 1).start()

    w1 = w1_ref[...].astype(jnp.bfloat16)
    z = jnp.dot(h_ref[...], w1, preferred_element_type=jnp.float32)
    z = jnp.square(jnp.maximum(z, 0.0)).astype(jnp.bfloat16)
    w2 = w2_ref[...].astype(jnp.bfloat16)
    o_ref[...] += jnp.dot(z, w2, preferred_element_type=jnp.float32)

    if apply_final_norm:
        @pl.when(f == pl.num_programs(1) - 1)
        def _():
            o_ref[...] = _rms_scale(o_ref[...]) * fg_ref[...]


def _mlp(x2d, g_all, w1_all, w2_all, layer, final_g, apply_final_norm):
    n, d = x2d.shape
    tm, tf = TM_MLP, TF_MLP
    assert n % tm == 0 and D_FF % tf == 0 and D_FF // tf >= 2
    return pl.pallas_call(
        functools.partial(_mlp_kernel, norm_layer=layer, apply_final_norm=apply_final_norm),
        name="mlp",
        grid=(n // tm, D_FF // tf),
        in_specs=[
            pl.BlockSpec(memory_space=pl.ANY),
            _const_spec(g_all.shape),
            pl.BlockSpec((None, d, tf), lambda i, f: (layer, 0, f)),
            pl.BlockSpec((None, tf, d), lambda i, f: (layer, f, 0)),
            _const_spec((1, d)),
        ],
        out_specs=pl.BlockSpec((tm, d), lambda i, f: (i, 0)),
        out_shape=jax.ShapeDtypeStruct(x2d.shape, x2d.dtype),
        scratch_shapes=[
            pltpu.VMEM((tm, d), jnp.float32),
            pltpu.VMEM((tm, d), jnp.bfloat16),
            pltpu.SemaphoreType.DMA(()),
        ],
        compiler_params=pltpu.CompilerParams(
            dimension_semantics=("arbitrary", "arbitrary"),
            vmem_limit_bytes=VMEM_LIMIT_BYTES),
    )(x2d, g_all, w1_all, w2_all, final_g.reshape(1, -1))


def kernel(x, norm_mix_g, norm_mlp_g, w_in_even, conv_a_w, conv_a_b, ln_a_g, ln_a_b,
           conv_b_w, w_out_even, pool_w, pool_scale, mlp_w1, mlp_w2, final_g):
    bsz, seq, d = x.shape
    depth = norm_mix_g.shape[0]
    n_odd, n_groups, gdim, _ = pool_w.shape
    pool_w_rows = pool_w.reshape(n_odd, n_groups * gdim, gdim)
    for layer in range(depth):
        i = layer // 2
        if layer % 2 == 0:
            x = _even_mixer(x, norm_mix_g, layer, w_in_even, conv_a_w, conv_a_b, ln_a_g, ln_a_b,
                            conv_b_w, w_out_even, i)
        else:
            x = _pool_mixer(x, norm_mix_g, layer, pool_w_rows, pool_scale, i)
        x = _mlp(x.reshape(bsz * seq, d), norm_mlp_g, mlp_w1, mlp_w2, layer, final_g,
                 apply_final_norm=(layer == depth - 1)).reshape(bsz, seq, d)
    return x
```

```python
import functools

import jax
import jax.numpy as jnp
from jax import lax
from jax.experimental import pallas as pl
from jax.experimental.pallas import tpu as pltpu

D_MODEL = 2048
D_A = D_MODEL // 2
D_B = D_MODEL // 2
D_IN_EVEN = 2 * D_A + 3 * D_B
CONV_A_WIDTH = 31
CONV_B_WIDTH = 3
POOL_WINDOWS = (2, 4, 8, 16)
POOL_GROUP_DIM = D_MODEL // len(POOL_WINDOWS)
D_FF = 4 * D_MODEL
RMS_EPS = 1e-6
LN_EPS = 1e-5

VMEM_LIMIT_BYTES = 56 * 1024 * 1024

A_HIST = 32
B_HIST = 8
POOL_HIST = 16

LANES = 128
ROW_PITCH = 2
CONV_ROWS = 128
RELEASE_LAG = 2
RELEASE_STRIDE = 2
WEIGHT_STAGE_SLOTS = 8
WEIGHT_CHUNK_BYTES = 3 << 19

TM_EVEN = 256
TM_POOL = 512
TM_MLP = 1024
TF_MLP = 512


def _rms_scale(x):
    ms = jnp.mean(x * x, axis=-1, keepdims=True)
    return x * lax.rsqrt(ms + RMS_EPS)


def _const_spec(shape):
    zeros = (0,) * len(shape)
    return pl.BlockSpec(shape, lambda *_: zeros, pipeline_mode=pl.Buffered(1))


def _stacked_spec(stacked, index):
    _, rows, cols = stacked.shape
    return pl.BlockSpec((None, rows, cols), lambda *_: (index, 0, 0), pipeline_mode=pl.Buffered(1))


def _slab_rows(hist, tm):
    return ROW_PITCH * (hist + tm)


def _slab_window(buf, c, token, rows):
    return buf[c, pl.ds(ROW_PITCH * token, rows, stride=ROW_PITCH), :]


def _slab_store(buf, hist, value):
    for c in range(value.shape[1] // LANES):
        _slab_store_chunk(buf, hist, c, value[:, c * LANES:(c + 1) * LANES])


def _slab_store_chunk(buf, hist, c, value):
    buf[c, pl.ds(ROW_PITCH * hist, value.shape[0], stride=ROW_PITCH), :] = value


def _slab_carry(buf, hist, tm, reset):
    for c in range(buf.shape[0]):
        tail = _slab_window(buf, c, tm, hist)
        buf[c, pl.ds(0, hist, stride=ROW_PITCH), :] = jnp.where(reset, 0.0, tail)


def _causal_depthwise_conv(buf, hist, w_ref, width, tm, bias_ref=None):
    outs = []
    for c in range(buf.shape[0]):
        lanes = slice(c * LANES, (c + 1) * LANES)
        parts = []
        for r0 in range(0, tm, CONV_ROWS):
            acc = None
            for k in range(width):
                window = _slab_window(buf, c, hist - (width - 1) + k + r0, CONV_ROWS)
                term = w_ref[k:k + 1, lanes] * window
                acc = term if acc is None else acc + term
            if bias_ref is not None:
                acc = acc + bias_ref[:, lanes]
            parts.append(acc)
        outs.append(jnp.concatenate(parts, axis=0))
    return jnp.concatenate(outs, axis=1)


def _fetch_weight_bf16(w_hbm, layer, w_vmem):
    k, n = w_vmem.shape
    rows_per_chunk = max(r for r in range(16, k + 1, 16)
                         if k % r == 0 and 4 * r * n <= WEIGHT_CHUNK_BYTES)
    n_chunks = k // rows_per_chunk
    n_slots = min(WEIGHT_STAGE_SLOTS, n_chunks)
    ahead = n_slots - 1
    assert ahead >= 1

    def body(stage, sem):
        def chunk_copy(c):
            slot = c % n_slots
            src = w_hbm.at[layer, pl.ds(c * rows_per_chunk, rows_per_chunk), :]
            return pltpu.make_async_copy(src, stage.at[slot], sem.at[slot])

        for c in range(ahead):
            chunk_copy(c).start()

        @pl.loop(0, n_chunks)
        def _(c):
            chunk_copy(c).wait()

            @pl.when(c + ahead < n_chunks)
            def _():
                chunk_copy(c + ahead).start()

            r0 = pl.multiple_of(c * rows_per_chunk, rows_per_chunk)
            w_vmem[pl.ds(r0, rows_per_chunk), :] = stage[c % n_slots].astype(jnp.bfloat16)

    pl.run_scoped(body, pltpu.VMEM((n_slots, rows_per_chunk, n), jnp.float32),
                  pltpu.SemaphoreType.DMA((n_slots,)))


def _even_mixer_kernel(x_ref, g_ref, win_hbm, caw_ref, cab_ref, lng_ref, lnb_ref,
                       cbw_ref, wout_hbm, o_ref, abuf, bbuf, win_ref, wout_ref, *, weight_layer):
    tm = x_ref.shape[1]

    @pl.when((pl.program_id(0) == 0) & (pl.program_id(1) == 0))
    def _():
        _fetch_weight_bf16(win_hbm, weight_layer, win_ref)
        _fetch_weight_bf16(wout_hbm, weight_layer, wout_ref)
        abuf[...] = jnp.zeros(abuf.shape, jnp.float32)
        bbuf[...] = jnp.zeros(bbuf.shape, jnp.float32)

    sequence_start = pl.program_id(1) == 0
    _slab_carry(abuf, A_HIST, tm, sequence_start)
    _slab_carry(bbuf, B_HIST, tm, sequence_start)

    x = x_ref[0]
    h = (_rms_scale(x) * g_ref[...]).astype(jnp.bfloat16)

    def proj(lo, width):
        return jnp.dot(h, win_ref[:, lo:lo + width], preferred_element_type=jnp.float32)

    piece = 2 * LANES
    n_pieces = D_A // piece
    chunks_per_piece = piece // LANES
    order = [(name, q) for q in range(n_pieces) for name in ("a_val", "a_gate")]
    order += [(name, q) for q in range(n_pieces) for name in ("b_x", "b_c")]
    order += [("b_b", q) for q in range(n_pieces)]
    first_col = {"a_val": 0, "a_gate": D_A, "b_x": 2 * D_A, "b_c": 2 * D_A + D_B,
                 "b_b": 2 * D_A + 2 * D_B}
    res, glu_chunks = {}, {}
    for idx, (name, q) in enumerate(order):
        r = proj(first_col[name] + q * piece, piece)
        res[name, q] = r
        if name == "a_gate":
            glu = res["a_val", q] * jax.nn.sigmoid(r)
            for half in range(chunks_per_piece):
                glu_chunks[q * chunks_per_piece + half] = glu[:, half * LANES:(half + 1) * LANES]
        if name == "b_c":
            cx = r * res["b_x", q]
            for half in range(chunks_per_piece):
                _slab_store_chunk(bbuf, B_HIST, q * chunks_per_piece + half,
                                  cx[:, half * LANES:(half + 1) * LANES])
        c, between = divmod(idx - RELEASE_LAG, RELEASE_STRIDE)
        if between == 0 and 0 <= c < D_A // LANES:
            released = jnp.where(r[:, :LANES] > jnp.inf, 0.0, glu_chunks[c])
            _slab_store_chunk(abuf, A_HIST, c, released)
    b_b = jnp.concatenate([res["b_b", q] for q in range(n_pieces)], axis=1)

    acc = _causal_depthwise_conv(abuf, A_HIST, caw_ref, CONV_A_WIDTH, tm, cab_ref)
    mu = jnp.mean(acc, axis=-1, keepdims=True)
    xc = acc - mu
    var = jnp.mean(xc * xc, axis=-1, keepdims=True)
    y = xc * lax.rsqrt(var + LN_EPS) * lng_ref[...] + lnb_ref[...]
    a_out = y * jax.nn.sigmoid(y)

    bo = b_b * _causal_depthwise_conv(bbuf, B_HIST, cbw_ref, CONV_B_WIDTH, tm)

    cat = jnp.concatenate([a_out, bo], axis=-1).astype(jnp.bfloat16)
    o_ref[0] = x + jnp.dot(cat, wout_ref[...], preferred_element_type=jnp.float32)


def _even_mixer(x, g_all, layer, w_in, conv_a_w, conv_a_b, ln_g, ln_b, conv_b_w, w_out, i):
    bsz, seq, d = x.shape
    tm = TM_EVEN
    return pl.pallas_call(
        functools.partial(_even_mixer_kernel, weight_layer=i),
        name="even_mixer",
        grid=(bsz, seq // tm),
        in_specs=[
            pl.BlockSpec((1, tm, d), lambda b, j: (b, j, 0)),
            _stacked_spec(g_all, layer),
            pl.BlockSpec(memory_space=pl.ANY),
            _stacked_spec(conv_a_w, i),
            _stacked_spec(conv_a_b, i),
            _stacked_spec(ln_g, i),
            _stacked_spec(ln_b, i),
            _stacked_spec(conv_b_w, i),
            pl.BlockSpec(memory_space=pl.ANY),
        ],
        out_specs=pl.BlockSpec((1, tm, d), lambda b, j: (b, j, 0)),
        out_shape=jax.ShapeDtypeStruct(x.shape, x.dtype),
        scratch_shapes=[
            pltpu.VMEM((D_A // LANES, _slab_rows(A_HIST, tm), LANES), jnp.float32),
            pltpu.VMEM((D_B // LANES, _slab_rows(B_HIST, tm), LANES), jnp.float32),
            pltpu.VMEM((d, D_IN_EVEN), jnp.bfloat16),
            pltpu.VMEM((D_A + D_B, d), jnp.bfloat16),
        ],
        compiler_params=pltpu.CompilerParams(
            dimension_semantics=("arbitrary", "arbitrary"),
            vmem_limit_bytes=VMEM_LIMIT_BYTES),
    )(x, g_all, w_in, conv_a_w, conv_a_b, ln_g, ln_b, conv_b_w, w_out)


def _pool_mixer_kernel(x_ref, g_ref, pw_hbm, ps_ref, o_ref, hbuf, lvl1, lvl2, lvl3, pw_ref, *,
                       weight_layer):
    tm = x_ref.shape[1]
    j = pl.program_id(1)
    chunks_per_group = POOL_GROUP_DIM // LANES
    levels = (hbuf, lvl1, lvl2, lvl3)
    assert POOL_WINDOWS == tuple(2 ** (g + 1) for g in range(len(levels)))

    @pl.when((pl.program_id(0) == 0) & (j == 0))
    def _():
        _fetch_weight_bf16(pw_hbm, weight_layer, pw_ref)
        for buf in levels:
            buf[...] = jnp.zeros(buf.shape, jnp.float32)

    for buf in levels:
        _slab_carry(buf, POOL_HIST, tm, j == 0)

    x = x_ref[0]
    h = _rms_scale(x) * g_ref[...]
    _slab_store(hbuf, POOL_HIST, h)
    pos = (j * tm + 1 + lax.broadcasted_iota(jnp.int32, (tm, 1), 0)).astype(jnp.float32)

    for grp, win in enumerate(POOL_WINDOWS):
        lo = grp * POOL_GROUP_DIM
        cols = slice(lo, lo + POOL_GROUP_DIM)
        n_levels = grp + 1
        sums = []
        for c in range(grp * chunks_per_group, (grp + 1) * chunks_per_group):
            s = h[:, c * LANES:(c + 1) * LANES]
            for k in range(1, n_levels + 1):
                src_chunk = c - (k - 1) * chunks_per_group
                s = s + _slab_window(levels[k - 1], src_chunk, POOL_HIST - 2 ** (k - 1), tm)
                if k < n_levels:
                    _slab_store_chunk(levels[k], POOL_HIST, c - k * chunks_per_group, s)
            sums.append(s)
        pooled = jnp.concatenate(sums, axis=1) / jnp.minimum(pos, float(win)) - h[:, cols]
        mixed = jnp.dot(pooled.astype(jnp.bfloat16), pw_ref[lo:lo + POOL_GROUP_DIM, :],
                        preferred_element_type=jnp.float32)
        o_ref[0, :, cols] = x[:, cols] + mixed * ps_ref[:, cols]


def _pool_mixer(x, g_all, layer, pool_w_rows, pool_scale_all, i):
    bsz, seq, d = x.shape
    tm = TM_POOL
    return pl.pallas_call(
        functools.partial(_pool_mixer_kernel, weight_layer=i),
        name="pool_mixer",
        grid=(bsz, seq // tm),
        in_specs=[
            pl.BlockSpec((1, tm, d), lambda b, j: (b, j, 0)),
            _stacked_spec(g_all, layer),
            pl.BlockSpec(memory_space=pl.ANY),
            _stacked_spec(pool_scale_all, i),
        ],
        out_specs=pl.BlockSpec((1, tm, d), lambda b, j: (b, j, 0)),
        out_shape=jax.ShapeDtypeStruct(x.shape, x.dtype),
        scratch_shapes=[
            *[pltpu.VMEM(((len(POOL_WINDOWS) - k) * POOL_GROUP_DIM // LANES,
                          _slab_rows(POOL_HIST, tm), LANES), jnp.float32)
              for k in range(len(POOL_WINDOWS))],
            pltpu.VMEM(pool_w_rows.shape[1:], jnp.bfloat16),
        ],
        compiler_params=pltpu.CompilerParams(
            dimension_semantics=("arbitrary", "arbitrary"),
            vmem_limit_bytes=VMEM_LIMIT_BYTES),
    )(x, g_all, pool_w_rows, pool_scale_all)


def _mlp_kernel(x_hbm, g_ref, w1_hbm, w2_hbm, fg_ref, o_hbm, h_ref, step_count, *, layer, n_tiles, n_ff,
                tm, tf, apply_final_norm):
    d = h_ref.shape[1]

    step_count[0] = 0

    def step(x_ref, w1_ref, w2_ref, o_ref):
        f = step_count[0] % n_ff
        step_count[0] += 1

        @pl.when(f == 0)
        def _():
            x = x_ref[...]
            h_ref[...] = (_rms_scale(x) * g_ref[layer]).astype(jnp.bfloat16)
            o_ref[...] = x

        w1 = w1_ref[...].astype(jnp.bfloat16)
        z = jnp.dot(h_ref[...], w1, preferred_element_type=jnp.float32)
        z = jnp.square(jnp.maximum(z, 0.0)).astype(jnp.bfloat16)
        w2 = w2_ref[...].astype(jnp.bfloat16)
        o_ref[...] += jnp.dot(z, w2, preferred_element_type=jnp.float32)

        if apply_final_norm:
            @pl.when(f == n_ff - 1)
            def _():
                o_ref[...] = _rms_scale(o_ref[...]) * fg_ref[...]

    pltpu.emit_pipeline(
        step,
        grid=(n_tiles, n_ff),
        in_specs=[
            pl.BlockSpec((tm, d), lambda i, f: (i, 0)),
            pl.BlockSpec((d, tf), lambda i, f: (0, f)),
            pl.BlockSpec((tf, d), lambda i, f: (f, 0)),
        ],
        out_specs=[pl.BlockSpec((tm, d), lambda i, f: (i, 0))],
    )(x_hbm, w1_hbm.at[layer], w2_hbm.at[layer], o_hbm)


def _mlp(x2d, g_all, w1_all, w2_all, layer, final_g, apply_final_norm):
    n, d = x2d.shape
    tm, tf = TM_MLP, TF_MLP
    assert n % tm == 0 and D_FF % tf == 0
    return pl.pallas_call(
        functools.partial(_mlp_kernel, layer=layer, n_tiles=n // tm, n_ff=D_FF // tf, tm=tm, tf=tf,
                          apply_final_norm=apply_final_norm),
        name="mlp",
        in_specs=[
            pl.BlockSpec(memory_space=pl.ANY),
            pl.BlockSpec(memory_space=pltpu.VMEM),
            pl.BlockSpec(memory_space=pl.ANY),
            pl.BlockSpec(memory_space=pl.ANY),
            pl.BlockSpec(memory_space=pltpu.VMEM),
        ],
        out_specs=pl.BlockSpec(memory_space=pl.ANY),
        out_shape=jax.ShapeDtypeStruct(x2d.shape, x2d.dtype),
        scratch_shapes=[pltpu.VMEM((tm, d), jnp.bfloat16), pltpu.SMEM((1,), jnp.int32)],
        compiler_params=pltpu.CompilerParams(vmem_limit_bytes=VMEM_LIMIT_BYTES),
    )(x2d, g_all, w1_all, w2_all, final_g.reshape(1, -1))


def kernel(x, norm_mix_g, norm_mlp_g, w_in_even, conv_a_w, conv_a_b, ln_a_g, ln_a_b,
           conv_b_w, w_out_even, pool_w, pool_scale, mlp_w1, mlp_w2, final_g):
    bsz, seq, d = x.shape
    depth = norm_mix_g.shape[0]
    n_odd, n_groups, gdim, _ = pool_w.shape
    pool_w_rows = pool_w.reshape(n_odd, n_groups * gdim, gdim)
    rows = lambda stacked: stacked.reshape(stacked.shape[0], 1, stacked.shape[1])
    mix_g, mlp_g = rows(norm_mix_g), rows(norm_mlp_g)
    conv_a_b, ln_a_g, ln_a_b, pool_scale = rows(conv_a_b), rows(ln_a_g), rows(ln_a_b), rows(pool_scale)
    for layer in range(depth):
        i = layer // 2
        if layer % 2 == 0:
            x = _even_mixer(x, mix_g, layer, w_in_even, conv_a_w, conv_a_b, ln_a_g, ln_a_b,
                            conv_b_w, w_out_even, i)
        else:
            x = _pool_mixer(x, mix_g, layer, pool_w_rows, pool_scale, i)
        x = _mlp(x.reshape(bsz * seq, d), mlp_g, mlp_w1, mlp_w2, layer, final_g,
                 apply_final_norm=(layer == depth - 1)).reshape(bsz, seq, d)
    return x
```

```python
import functools

import jax
import jax.numpy as jnp
from jax import lax
from jax.experimental import pallas as pl
from jax.experimental.pallas import tpu as pltpu

D_MODEL = 2048
D_A = D_MODEL // 2
D_B = D_MODEL // 2
D_IN_EVEN = 2 * D_A + 3 * D_B
CONV_A_WIDTH = 31
CONV_B_WIDTH = 3
POOL_WINDOWS = (2, 4, 8, 16)
POOL_GROUP_DIM = D_MODEL // len(POOL_WINDOWS)
D_FF = 4 * D_MODEL
RMS_EPS = 1e-6
LN_EPS = 1e-5

VMEM_LIMIT_BYTES = 56 * 1024 * 1024

A_HIST = 32
B_HIST = 8
POOL_HIST = 16

LANES = 128
ROW_PITCH = 2
CONV_ROWS = 128
RELEASE_LAG = 2
RELEASE_STRIDE = 2
WEIGHT_STAGE_SLOTS = 8
WEIGHT_CHUNK_BYTES = 3 << 19

TM_EVEN = 256
TM_POOL = 512
TM_MLP = 1024
TF_MLP = 512


def _rms_scale(x):
    ms = jnp.mean(x * x, axis=-1, keepdims=True)
    return x * lax.rsqrt(ms + RMS_EPS)


def _const_spec(shape):
    zeros = (0,) * len(shape)
    return pl.BlockSpec(shape, lambda *_: zeros, pipeline_mode=pl.Buffered(1))


def _stacked_spec(stacked, index):
    _, rows, cols = stacked.shape
    return pl.BlockSpec((None, rows, cols), lambda *_: (index, 0, 0), pipeline_mode=pl.Buffered(1))


def _slab_rows(hist, tm):
    return ROW_PITCH * (hist + tm)


def _slab_window(buf, c, token, rows):
    return buf[c, pl.ds(ROW_PITCH * token, rows, stride=ROW_PITCH), :]


def _slab_store(buf, hist, value):
    for c in range(value.shape[1] // LANES):
        _slab_store_chunk(buf, hist, c, value[:, c * LANES:(c + 1) * LANES])


def _slab_store_chunk(buf, hist, c, value):
    buf[c, pl.ds(ROW_PITCH * hist, value.shape[0], stride=ROW_PITCH), :] = value


def _slab_carry(buf, hist, tm, reset):
    for c in range(buf.shape[0]):
        tail = _slab_window(buf, c, tm, hist)
        buf[c, pl.ds(0, hist, stride=ROW_PITCH), :] = jnp.where(reset, 0.0, tail)


def _causal_depthwise_conv(buf, hist, w_ref, width, tm, bias_ref=None):
    outs = []
    for c in range(buf.shape[0]):
        lanes = slice(c * LANES, (c + 1) * LANES)
        parts = []
        for r0 in range(0, tm, CONV_ROWS):
            acc = None
            for k in range(width):
                window = _slab_window(buf, c, hist - (width - 1) + k + r0, CONV_ROWS)
                term = w_ref[k:k + 1, lanes] * window
                acc = term if acc is None else acc + term
            if bias_ref is not None:
                acc = acc + bias_ref[:, lanes]
            parts.append(acc)
        outs.append(jnp.concatenate(parts, axis=0))
    return jnp.concatenate(outs, axis=1)


def _fetch_weight_bf16(w_hbm, layer, w_vmem):
    k, n = w_vmem.shape
    rows_per_chunk = max(r for r in range(16, k + 1, 16)
                         if k % r == 0 and 4 * r * n <= WEIGHT_CHUNK_BYTES)
    n_chunks = k // rows_per_chunk
    n_slots = min(WEIGHT_STAGE_SLOTS, n_chunks)
    ahead = n_slots - 1
    assert ahead >= 1

    def body(stage, sem):
        def chunk_copy(c):
            slot = c % n_slots
            src = w_hbm.at[layer, pl.ds(c * rows_per_chunk, rows_per_chunk), :]
            return pltpu.make_async_copy(src, stage.at[slot], sem.at[slot])

        for c in range(ahead):
            chunk_copy(c).start()

        @pl.loop(0, n_chunks)
        def _(c):
            chunk_copy(c).wait()

            @pl.when(c + ahead < n_chunks)
            def _():
                chunk_copy(c + ahead).start()

            r0 = pl.multiple_of(c * rows_per_chunk, rows_per_chunk)
            w_vmem[pl.ds(r0, rows_per_chunk), :] = stage[c % n_slots].astype(jnp.bfloat16)

    pl.run_scoped(body, pltpu.VMEM((n_slots, rows_per_chunk, n), jnp.float32),
                  pltpu.SemaphoreType.DMA((n_slots,)))


def _even_mixer_kernel(x_ref, g_ref, win_hbm, caw_ref, cab_ref, lng_ref, lnb_ref,
                       cbw_ref, wout_hbm, o_ref, abuf, bbuf, win_ref, wout_ref, *, weight_layer):
    tm = x_ref.shape[1]

    @pl.when((pl.program_id(0) == 0) & (pl.program_id(1) == 0))
    def _():
        _fetch_weight_bf16(win_hbm, weight_layer, win_ref)
        _fetch_weight_bf16(wout_hbm, weight_layer, wout_ref)
        abuf[...] = jnp.zeros(abuf.shape, jnp.float32)
        bbuf[...] = jnp.zeros(bbuf.shape, jnp.float32)

    sequence_start = pl.program_id(1) == 0
    _slab_carry(abuf, A_HIST, tm, sequence_start)
    _slab_carry(bbuf, B_HIST, tm, sequence_start)

    x = x_ref[0]
    h = (_rms_scale(x) * g_ref[...]).astype(jnp.bfloat16)

    def proj(lo, width):
        return jnp.dot(h, win_ref[:, lo:lo + width], preferred_element_type=jnp.float32)

    piece = 2 * LANES
    n_pieces = D_A // piece
    chunks_per_piece = piece // LANES
    order = [(name, q) for q in range(n_pieces) for name in ("a_val", "a_gate")]
    order += [(name, q) for q in range(n_pieces) for name in ("b_x", "b_c")]
    order += [("b_b", q) for q in range(n_pieces)]
    first_col = {"a_val": 0, "a_gate": D_A, "b_x": 2 * D_A, "b_c": 2 * D_A + D_B,
                 "b_b": 2 * D_A + 2 * D_B}
    res, glu_chunks = {}, {}
    for idx, (name, q) in enumerate(order):
        r = proj(first_col[name] + q * piece, piece)
        res[name, q] = r
        if name == "a_gate":
            glu = res["a_val", q] * jax.nn.sigmoid(r)
            for half in range(chunks_per_piece):
                glu_chunks[q * chunks_per_piece + half] = glu[:, half * LANES:(half + 1) * LANES]
        if name == "b_c":
            cx = r * res["b_x", q]
            for half in range(chunks_per_piece):
                _slab_store_chunk(bbuf, B_HIST, q * chunks_per_piece + half,
                                  cx[:, half * LANES:(half + 1) * LANES])
        c, between = divmod(idx - RELEASE_LAG, RELEASE_STRIDE)
        if between == 0 and 0 <= c < D_A // LANES:
            released = jnp.where(r[:, :LANES] > jnp.inf, 0.0, glu_chunks[c])
            _slab_store_chunk(abuf, A_HIST, c, released)
    b_b = jnp.concatenate([res["b_b", q] for q in range(n_pieces)], axis=1)

    acc = _causal_depthwise_conv(abuf, A_HIST, caw_ref, CONV_A_WIDTH, tm, cab_ref)
    mu = jnp.mean(acc, axis=-1, keepdims=True)
    xc = acc - mu
    var = jnp.mean(xc * xc, axis=-1, keepdims=True)
    y = xc * lax.rsqrt(var + LN_EPS) * lng_ref[...] + lnb_ref[...]
    a_out = y * jax.nn.sigmoid(y)

    bo = b_b * _causal_depthwise_conv(bbuf, B_HIST, cbw_ref, CONV_B_WIDTH, tm)

    cat = jnp.concatenate([a_out, bo], axis=-1).astype(jnp.bfloat16)
    o_ref[0] = x + jnp.dot(cat, wout_ref[...], preferred_element_type=jnp.float32)


def _even_mixer(x, g_all, layer, w_in, conv_a_w, conv_a_b, ln_g, ln_b, conv_b_w, w_out, i):
    bsz, seq, d = x.shape
    tm = TM_EVEN
    return pl.pallas_call(
        functools.partial(_even_mixer_kernel, weight_layer=i),
        name="even_mixer",
        grid=(bsz, seq // tm),
        in_specs=[
            pl.BlockSpec((1, tm, d), lambda b, j: (b, j, 0)),
            _stacked_spec(g_all, layer),
            pl.BlockSpec(memory_space=pl.ANY),
            _stacked_spec(conv_a_w, i),
            _stacked_spec(conv_a_b, i),
            _stacked_spec(ln_g, i),
            _stacked_spec(ln_b, i),
            _stacked_spec(conv_b_w, i),
            pl.BlockSpec(memory_space=pl.ANY),
        ],
        out_specs=pl.BlockSpec((1, tm, d), lambda b, j: (b, j, 0)),
        out_shape=jax.ShapeDtypeStruct(x.shape, x.dtype),
        scratch_shapes=[
            pltpu.VMEM((D_A // LANES, _slab_rows(A_HIST, tm), LANES), jnp.float32),
            pltpu.VMEM((D_B // LANES, _slab_rows(B_HIST, tm), LANES), jnp.float32),
            pltpu.VMEM((d, D_IN_EVEN), jnp.bfloat16),
            pltpu.VMEM((D_A + D_B, d), jnp.bfloat16),
        ],
        compiler_params=pltpu.CompilerParams(
            dimension_semantics=("arbitrary", "arbitrary"),
            vmem_limit_bytes=VMEM_LIMIT_BYTES),
    )(x, g_all, w_in, conv_a_w, conv_a_b, ln_g, ln_b, conv_b_w, w_out)


def _pool_mixer_kernel(x_hbm, g_all_ref, pw_hbm, ps_all_ref, o_hbm, hbuf, lvl1, lvl2, lvl3, pw_ref,
                       step_count, *, weight_layer, norm_layer, grid, tm):
    g_ref, ps_ref = g_all_ref.at[norm_layer], ps_all_ref.at[weight_layer]
    levels = (hbuf, lvl1, lvl2, lvl3)
    assert POOL_WINDOWS == tuple(2 ** (g + 1) for g in range(len(levels)))

    _fetch_weight_bf16(pw_hbm, weight_layer, pw_ref)
    for buf in levels:
        buf[...] = jnp.zeros(buf.shape, jnp.float32)
    step_count[0] = 0

    def tile(x_ref, o_ref):
        j = step_count[0] % grid[1]
        step_count[0] += 1
        _pool_tile(x_ref, o_ref, g_ref, ps_ref, levels, pw_ref, j, tm)

    block = pl.BlockSpec((1, tm, x_hbm.shape[2]), lambda b, j: (b, j, 0))
    pltpu.emit_pipeline(
        tile, grid=grid,
        in_specs=[pl.BlockSpec(block.block_shape, block.index_map, pipeline_mode=pl.Buffered(3))],
        out_specs=[block],
    )(x_hbm, o_hbm)


def _pool_tile(x_ref, o_ref, g_ref, ps_ref, levels, pw_ref, j, tm):
    hbuf = levels[0]
    chunks_per_group = POOL_GROUP_DIM // LANES
    for buf in levels:
        _slab_carry(buf, POOL_HIST, tm, j == 0)

    x = x_ref[0]
    h = _rms_scale(x) * g_ref[...]
    _slab_store(hbuf, POOL_HIST, h)
    pos = (j * tm + 1 + lax.broadcasted_iota(jnp.int32, (tm, 1), 0)).astype(jnp.float32)

    for grp, win in enumerate(POOL_WINDOWS):
        lo = grp * POOL_GROUP_DIM
        cols = slice(lo, lo + POOL_GROUP_DIM)
        n_levels = grp + 1
        sums = []
        for c in range(grp * chunks_per_group, (grp + 1) * chunks_per_group):
            s = h[:, c * LANES:(c + 1) * LANES]
            for k in range(1, n_levels + 1):
                src_chunk = c - (k - 1) * chunks_per_group
                s = s + _slab_window(levels[k - 1], src_chunk, POOL_HIST - 2 ** (k - 1), tm)
                if k < n_levels:
                    _slab_store_chunk(levels[k], POOL_HIST, c - k * chunks_per_group, s)
            sums.append(s)
        pooled = jnp.concatenate(sums, axis=1) / jnp.minimum(pos, float(win)) - h[:, cols]
        mixed = jnp.dot(pooled.astype(jnp.bfloat16), pw_ref[lo:lo + POOL_GROUP_DIM, :],
                        preferred_element_type=jnp.float32)
        o_ref[0, :, cols] = x[:, cols] + mixed * ps_ref[:, cols]


def _pool_mixer(x, g_all, layer, pool_w_rows, pool_scale_all, i):
    bsz, seq, d = x.shape
    tm = TM_POOL
    return pl.pallas_call(
        functools.partial(_pool_mixer_kernel, weight_layer=i, norm_layer=layer,
                          grid=(bsz, seq // tm), tm=tm),
        name="pool_mixer",
        in_specs=[
            pl.BlockSpec(memory_space=pl.ANY),
            pl.BlockSpec(memory_space=pltpu.VMEM),
            pl.BlockSpec(memory_space=pl.ANY),
            pl.BlockSpec(memory_space=pltpu.VMEM),
        ],
        out_specs=pl.BlockSpec(memory_space=pl.ANY),
        out_shape=jax.ShapeDtypeStruct(x.shape, x.dtype),
        scratch_shapes=[
            *[pltpu.VMEM(((len(POOL_WINDOWS) - k) * POOL_GROUP_DIM // LANES,
                          _slab_rows(POOL_HIST, tm), LANES), jnp.float32)
              for k in range(len(POOL_WINDOWS))],
            pltpu.VMEM(pool_w_rows.shape[1:], jnp.bfloat16),
            pltpu.SMEM((1,), jnp.int32),
        ],
        compiler_params=pltpu.CompilerParams(vmem_limit_bytes=VMEM_LIMIT_BYTES),
    )(x, g_all, pool_w_rows, pool_scale_all)


def _mlp_kernel(x_hbm, g_ref, w1_ref, w2_ref, fg_ref, o_ref, xbuf, h_ref, sem, *,
                apply_final_norm):
    i, f = pl.program_id(0), pl.program_id(1)
    tm = o_ref.shape[0]

    def x_copy(tile):
        return pltpu.make_async_copy(x_hbm.at[pl.ds(tile * tm, tm), :], xbuf, sem)

    @pl.when(f == 0)
    def _():
        @pl.when(i == 0)
        def _():
            x_copy(0).start()

        x_copy(i).wait()
        x = xbuf[...]
        h_ref[...] = (_rms_scale(x) * g_ref[...]).astype(jnp.bfloat16)
        o_ref[...] = x

    @pl.when((f == 1) & (i + 1 < pl.num_programs(0)))
    def _():
        x_copy(i + 1).start()

    w1 = w1_ref[...].astype(jnp.bfloat16)
    z = jnp.dot(h_ref[...], w1, preferred_element_type=jnp.float32)
    z = jnp.square(jnp.maximum(z, 0.0)).astype(jnp.bfloat16)
    w2 = w2_ref[...].astype(jnp.bfloat16)
    o_ref[...] += jnp.dot(z, w2, preferred_element_type=jnp.float32)

    if apply_final_norm:
        @pl.when(f == pl.num_programs(1) - 1)
        def _():
            o_ref[...] = _rms_scale(o_ref[...]) * fg_ref[...]


def _mlp(x2d, g_all, w1_all, w2_all, layer, final_g, apply_final_norm):
    n, d = x2d.shape
    tm, tf = TM_MLP, TF_MLP
    assert n % tm == 0 and D_FF % tf == 0 and D_FF // tf >= 2
    return pl.pallas_call(
        functools.partial(_mlp_kernel, apply_final_norm=apply_final_norm),
        name="mlp",
        grid=(n // tm, D_FF // tf),
        in_specs=[
            pl.BlockSpec(memory_space=pl.ANY),
            _stacked_spec(g_all, layer),
            pl.BlockSpec((None, d, tf), lambda i, f: (layer, 0, f)),
            pl.BlockSpec((None, tf, d), lambda i, f: (layer, f, 0)),
            _const_spec((1, d)),
        ],
        out_specs=pl.BlockSpec((tm, d), lambda i, f: (i, 0)),
        out_shape=jax.ShapeDtypeStruct(x2d.shape, x2d.dtype),
        scratch_shapes=[
            pltpu.VMEM((tm, d), jnp.float32),
            pltpu.VMEM((tm, d), jnp.bfloat16),
            pltpu.SemaphoreType.DMA(()),
        ],
        compiler_params=pltpu.CompilerParams(
            dimension_semantics=("arbitrary", "arbitrary"),
            vmem_limit_bytes=VMEM_LIMIT_BYTES),
    )(x2d, g_all, w1_all, w2_all, final_g.reshape(1, -1))


def kernel(x, norm_mix_g, norm_mlp_g, w_in_even, conv_a_w, conv_a_b, ln_a_g, ln_a_b,
           conv_b_w, w_out_even, pool_w, pool_scale, mlp_w1, mlp_w2, final_g):
    bsz, seq, d = x.shape
    depth = norm_mix_g.shape[0]
    n_odd, n_groups, gdim, _ = pool_w.shape
    pool_w_rows = pool_w.reshape(n_odd, n_groups * gdim, gdim)
    rows = lambda stacked: stacked.reshape(stacked.shape[0], 1, stacked.shape[1])
    mix_g, mlp_g = rows(norm_mix_g), rows(norm_mlp_g)
    conv_a_b, ln_a_g, ln_a_b, pool_scale = rows(conv_a_b), rows(ln_a_g), rows(ln_a_b), rows(pool_scale)
    for layer in range(depth):
        i = layer // 2
        if layer % 2 == 0:
            x = _even_mixer(x, mix_g, layer, w_in_even, conv_a_w, conv_a_b, ln_a_g, ln_a_b,
                            conv_b_w, w_out_even, i)
        else:
            x = _pool_mixer(x, mix_g, layer, pool_w_rows, pool_scale, i)
        x = _mlp(x.reshape(bsz * seq, d), mlp_g, mlp_w1, mlp_w2, layer, final_g,
                 apply_final_norm=(layer == depth - 1)).reshape(bsz, seq, d)
    return x
```

```python
import functools

import jax
import jax.numpy as jnp
from jax import lax
from jax.experimental import pallas as pl
from jax.experimental.pallas import tpu as pltpu

D_MODEL = 2048
D_A = D_MODEL // 2
D_B = D_MODEL // 2
D_IN_EVEN = 2 * D_A + 3 * D_B
CONV_A_WIDTH = 31
CONV_B_WIDTH = 3
POOL_WINDOWS = (2, 4, 8, 16)
POOL_GROUP_DIM = D_MODEL // len(POOL_WINDOWS)
D_FF = 4 * D_MODEL
RMS_EPS = 1e-6
LN_EPS = 1e-5

VMEM_LIMIT_BYTES = 56 * 1024 * 1024

A_HIST = 32
B_HIST = 8
POOL_HIST = 16

LANES = 128
ROW_PITCH = 2
CONV_ROWS = 128
RELEASE_LAG = 2
RELEASE_STRIDE = 2
WEIGHT_STAGE_SLOTS = 8
WEIGHT_CHUNK_BYTES = 3 << 19

TM_EVEN = 256
TM_POOL = 512
TM_MLP = 1024
TF_MLP = 512


def _rms_scale(x):
    ms = jnp.mean(x * x, axis=-1, keepdims=True)
    return x * lax.rsqrt(ms + RMS_EPS)


def _const_spec(shape):
    zeros = (0,) * len(shape)
    return pl.BlockSpec(shape, lambda *_: zeros, pipeline_mode=pl.Buffered(1))


def _stacked_spec(stacked, index):
    _, rows, cols = stacked.shape
    return pl.BlockSpec((None, rows, cols), lambda *_: (index, 0, 0), pipeline_mode=pl.Buffered(1))


def _slab_rows(hist, tm):
    return ROW_PITCH * (hist + tm)


def _slab_window(buf, c, token, rows):
    return buf[c, pl.ds(ROW_PITCH * token, rows, stride=ROW_PITCH), :]


def _slab_store(buf, hist, value):
    for c in range(value.shape[1] // LANES):
        _slab_store_chunk(buf, hist, c, value[:, c * LANES:(c + 1) * LANES])


def _slab_store_chunk(buf, hist, c, value):
    buf[c, pl.ds(ROW_PITCH * hist, value.shape[0], stride=ROW_PITCH), :] = value


def _slab_carry(buf, hist, tm, reset):
    for c in range(buf.shape[0]):
        tail = _slab_window(buf, c, tm, hist)
        buf[c, pl.ds(0, hist, stride=ROW_PITCH), :] = jnp.where(reset, 0.0, tail)


def _causal_depthwise_conv(buf, hist, w_ref, width, tm, bias_ref=None):
    outs = []
    for c in range(buf.shape[0]):
        lanes = slice(c * LANES, (c + 1) * LANES)
        parts = []
        for r0 in range(0, tm, CONV_ROWS):
            acc = None
            for k in range(width):
                window = _slab_window(buf, c, hist - (width - 1) + k + r0, CONV_ROWS)
                term = w_ref[k:k + 1, lanes] * window
                acc = term if acc is None else acc + term
            if bias_ref is not None:
                acc = acc + bias_ref[:, lanes]
            parts.append(acc)
        outs.append(jnp.concatenate(parts, axis=0))
    return jnp.concatenate(outs, axis=1)


def _fetch_weight_bf16(w_hbm, layer, w_vmem):
    k, n = w_vmem.shape
    rows_per_chunk = max(r for r in range(16, k + 1, 16)
                         if k % r == 0 and 4 * r * n <= WEIGHT_CHUNK_BYTES)
    n_chunks = k // rows_per_chunk
    n_slots = min(WEIGHT_STAGE_SLOTS, n_chunks)
    ahead = n_slots - 1
    assert ahead >= 1

    def body(stage, sem):
        def chunk_copy(c):
            slot = c % n_slots
            src = w_hbm.at[layer, pl.ds(c * rows_per_chunk, rows_per_chunk), :]
            return pltpu.make_async_copy(src, stage.at[slot], sem.at[slot])

        for c in range(ahead):
            chunk_copy(c).start()

        @pl.loop(0, n_chunks)
        def _(c):
            chunk_copy(c).wait()

            @pl.when(c + ahead < n_chunks)
            def _():
                chunk_copy(c + ahead).start()

            r0 = pl.multiple_of(c * rows_per_chunk, rows_per_chunk)
            w_vmem[pl.ds(r0, rows_per_chunk), :] = stage[c % n_slots].astype(jnp.bfloat16)

    pl.run_scoped(body, pltpu.VMEM((n_slots, rows_per_chunk, n), jnp.float32),
                  pltpu.SemaphoreType.DMA((n_slots,)))


def _even_mixer_kernel(x_hbm, g_all, win_hbm, caw_all, cab_all, lng_all, lnb_all,
                       cbw_all, wout_hbm, o_hbm, abuf, bbuf, win_ref, wout_ref, step_count, *,
                       weight_layer, norm_layer, grid, tm):
    g_ref = g_all.at[norm_layer]
    caw_ref, cab_ref, lng_ref, lnb_ref, cbw_ref = (
        r.at[weight_layer] for r in (caw_all, cab_all, lng_all, lnb_all, cbw_all))

    _fetch_weight_bf16(win_hbm, weight_layer, win_ref)
    _fetch_weight_bf16(wout_hbm, weight_layer, wout_ref)
    abuf[...] = jnp.zeros(abuf.shape, jnp.float32)
    bbuf[...] = jnp.zeros(bbuf.shape, jnp.float32)
    step_count[0] = 0

    def tile(x_ref, o_ref):
        j = step_count[0] % grid[1]
        step_count[0] += 1
        _even_tile(x_ref, o_ref, g_ref, caw_ref, cab_ref, lng_ref, lnb_ref, cbw_ref, abuf, bbuf,
                   win_ref, wout_ref, j, tm)

    block = pl.BlockSpec((1, tm, x_hbm.shape[2]), lambda b, j: (b, j, 0))
    pltpu.emit_pipeline(
        tile, grid=grid,
        in_specs=[pl.BlockSpec(block.block_shape, block.index_map, pipeline_mode=pl.Buffered(3))],
        out_specs=[block],
    )(x_hbm, o_hbm)


def _even_tile(x_ref, o_ref, g_ref, caw_ref, cab_ref, lng_ref, lnb_ref, cbw_ref, abuf, bbuf,
               win_ref, wout_ref, j, tm):
    sequence_start = j == 0
    _slab_carry(abuf, A_HIST, tm, sequence_start)
    _slab_carry(bbuf, B_HIST, tm, sequence_start)

    x = x_ref[0]
    h = (_rms_scale(x) * g_ref[...]).astype(jnp.bfloat16)

    def proj(lo, width):
        return jnp.dot(h, win_ref[:, lo:lo + width], preferred_element_type=jnp.float32)

    piece = 2 * LANES
    n_pieces = D_A // piece
    chunks_per_piece = piece // LANES
    order = [(name, q) for q in range(n_pieces) for name in ("a_val", "a_gate")]
    order += [(name, q) for q in range(n_pieces) for name in ("b_x", "b_c")]
    order += [("b_b", q) for q in range(n_pieces)]
    first_col = {"a_val": 0, "a_gate": D_A, "b_x": 2 * D_A, "b_c": 2 * D_A + D_B,
                 "b_b": 2 * D_A + 2 * D_B}
    res, glu_chunks = {}, {}
    for idx, (name, q) in enumerate(order):
        r = proj(first_col[name] + q * piece, piece)
        res[name, q] = r
        if name == "a_gate":
            glu = res["a_val", q] * jax.nn.sigmoid(r)
            for half in range(chunks_per_piece):
                glu_chunks[q * chunks_per_piece + half] = glu[:, half * LANES:(half + 1) * LANES]
        if name == "b_c":
            cx = r * res["b_x", q]
            for half in range(chunks_per_piece):
                _slab_store_chunk(bbuf, B_HIST, q * chunks_per_piece + half,
                                  cx[:, half * LANES:(half + 1) * LANES])
        c, between = divmod(idx - RELEASE_LAG, RELEASE_STRIDE)
        if between == 0 and 0 <= c < D_A // LANES:
            released = jnp.where(r[:, :LANES] > jnp.inf, 0.0, glu_chunks[c])
            _slab_store_chunk(abuf, A_HIST, c, released)
    b_b = jnp.concatenate([res["b_b", q] for q in range(n_pieces)], axis=1)

    acc = _causal_depthwise_conv(abuf, A_HIST, caw_ref, CONV_A_WIDTH, tm, cab_ref)
    mu = jnp.mean(acc, axis=-1, keepdims=True)
    xc = acc - mu
    var = jnp.mean(xc * xc, axis=-1, keepdims=True)
    y = xc * lax.rsqrt(var + LN_EPS) * lng_ref[...] + lnb_ref[...]
    a_out = y * jax.nn.sigmoid(y)

    bo = b_b * _causal_depthwise_conv(bbuf, B_HIST, cbw_ref, CONV_B_WIDTH, tm)

    cat = jnp.concatenate([a_out, bo], axis=-1).astype(jnp.bfloat16)
    o_ref[0] = x + jnp.dot(cat, wout_ref[...], preferred_element_type=jnp.float32)


def _even_mixer(x, g_all, layer, w_in, conv_a_w, conv_a_b, ln_g, ln_b, conv_b_w, w_out, i):
    bsz, seq, d = x.shape
    tm = TM_EVEN
    return pl.pallas_call(
        functools.partial(_even_mixer_kernel, weight_layer=i, norm_layer=layer,
                          grid=(bsz, seq // tm), tm=tm),
        name="even_mixer",
        in_specs=[
            pl.BlockSpec(memory_space=pl.ANY),
            pl.BlockSpec(memory_space=pltpu.VMEM),
            pl.BlockSpec(memory_space=pl.ANY),
            pl.BlockSpec(memory_space=pltpu.VMEM),
            pl.BlockSpec(memory_space=pltpu.VMEM),
            pl.BlockSpec(memory_space=pltpu.VMEM),
            pl.BlockSpec(memory_space=pltpu.VMEM),
            pl.BlockSpec(memory_space=pltpu.VMEM),
            pl.BlockSpec(memory_space=pl.ANY),
        ],
        out_specs=pl.BlockSpec(memory_space=pl.ANY),
        out_shape=jax.ShapeDtypeStruct(x.shape, x.dtype),
        scratch_shapes=[
            pltpu.VMEM((D_A // LANES, _slab_rows(A_HIST, tm), LANES), jnp.float32),
            pltpu.VMEM((D_B // LANES, _slab_rows(B_HIST, tm), LANES), jnp.float32),
            pltpu.VMEM((d, D_IN_EVEN), jnp.bfloat16),
            pltpu.VMEM((D_A + D_B, d), jnp.bfloat16),
            pltpu.SMEM((1,), jnp.int32),
        ],
        compiler_params=pltpu.CompilerParams(vmem_limit_bytes=VMEM_LIMIT_BYTES),
    )(x, g_all, w_in, conv_a_w, conv_a_b, ln_g, ln_b, conv_b_w, w_out)


def _pool_mixer_kernel(x_hbm, g_all_ref, pw_hbm, ps_all_ref, o_hbm, hbuf, lvl1, lvl2, lvl3, pw_ref,
                       step_count, *, weight_layer, norm_layer, grid, tm):
    g_ref, ps_ref = g_all_ref.at[norm_layer], ps_all_ref.at[weight_layer]
    levels = (hbuf, lvl1, lvl2, lvl3)
    assert POOL_WINDOWS == tuple(2 ** (g + 1) for g in range(len(levels)))

    _fetch_weight_bf16(pw_hbm, weight_layer, pw_ref)
    for buf in levels:
        buf[...] = jnp.zeros(buf.shape, jnp.float32)
    step_count[0] = 0

    def tile(x_ref, o_ref):
        j = step_count[0] % grid[1]
        step_count[0] += 1
        _pool_tile(x_ref, o_ref, g_ref, ps_ref, levels, pw_ref, j, tm)

    block = pl.BlockSpec((1, tm, x_hbm.shape[2]), lambda b, j: (b, j, 0))
    pltpu.emit_pipeline(
        tile, grid=grid,
        in_specs=[pl.BlockSpec(block.block_shape, block.index_map, pipeline_mode=pl.Buffered(3))],
        out_specs=[block],
    )(x_hbm, o_hbm)


def _pool_tile(x_ref, o_ref, g_ref, ps_ref, levels, pw_ref, j, tm):
    hbuf = levels[0]
    chunks_per_group = POOL_GROUP_DIM // LANES
    for buf in levels:
        _slab_carry(buf, POOL_HIST, tm, j == 0)

    x = x_ref[0]
    h = _rms_scale(x) * g_ref[...]
    _slab_store(hbuf, POOL_HIST, h)
    pos = (j * tm + 1 + lax.broadcasted_iota(jnp.int32, (tm, 1), 0)).astype(jnp.float32)

    for grp, win in enumerate(POOL_WINDOWS):
        lo = grp * POOL_GROUP_DIM
        cols = slice(lo, lo + POOL_GROUP_DIM)
        n_levels = grp + 1
        sums = []
        for c in range(grp * chunks_per_group, (grp + 1) * chunks_per_group):
            s = h[:, c * LANES:(c + 1) * LANES]
            for k in range(1, n_levels + 1):
                src_chunk = c - (k - 1) * chunks_per_group
                s = s + _slab_window(levels[k - 1], src_chunk, POOL_HIST - 2 ** (k - 1), tm)
                if k < n_levels:
                    _slab_store_chunk(levels[k], POOL_HIST, c - k * chunks_per_group, s)
            sums.append(s)
        pooled = jnp.concatenate(sums, axis=1) / jnp.minimum(pos, float(win)) - h[:, cols]
        mixed = jnp.dot(pooled.astype(jnp.bfloat16), pw_ref[lo:lo + POOL_GROUP_DIM, :],
                        preferred_element_type=jnp.float32)
        o_ref[0, :, cols] = x[:, cols] + mixed * ps_ref[:, cols]


def _pool_mixer(x, g_all, layer, pool_w_rows, pool_scale_all, i):
    bsz, seq, d = x.shape
    tm = TM_POOL
    return pl.pallas_call(
        functools.partial(_pool_mixer_kernel, weight_layer=i, norm_layer=layer,
                          grid=(bsz, seq // tm), tm=tm),
        name="pool_mixer",
        in_specs=[
            pl.BlockSpec(memory_space=pl.ANY),
            pl.BlockSpec(memory_space=pltpu.VMEM),
            pl.BlockSpec(memory_space=pl.ANY),
            pl.BlockSpec(memory_space=pltpu.VMEM),
        ],
        out_specs=pl.BlockSpec(memory_space=pl.ANY),
        out_shape=jax.ShapeDtypeStruct(x.shape, x.dtype),
        scratch_shapes=[
            *[pltpu.VMEM(((len(POOL_WINDOWS) - k) * POOL_GROUP_DIM // LANES,
                          _slab_rows(POOL_HIST, tm), LANES), jnp.float32)
              for k in range(len(POOL_WINDOWS))],
            pltpu.VMEM(pool_w_rows.shape[1:], jnp.bfloat16),
            pltpu.SMEM((1,), jnp.int32),
        ],
        compiler_params=pltpu.CompilerParams(vmem_limit_bytes=VMEM_LIMIT_BYTES),
    )(x, g_all, pool_w_rows, pool_scale_all)


def _mlp_kernel(x_hbm, g_ref, w1_ref, w2_ref, fg_ref, o_ref, xbuf, h_ref, sem, *,
                apply_final_norm):
    i, f = pl.program_id(0), pl.program_id(1)
    tm = o_ref.shape[0]

    def x_copy(tile):
        return pltpu.make_async_copy(x_hbm.at[pl.ds(tile * tm, tm), :], xbuf, sem)

    @pl.when(f == 0)
    def _():
        @pl.when(i == 0)
        def _():
            x_copy(0).start()

        x_copy(i).wait()
        x = xbuf[...]
        h_ref[...] = (_rms_scale(x) * g_ref[...]).astype(jnp.bfloat16)
        o_ref[...] = x

    @pl.when((f == 1) & (i + 1 < pl.num_programs(0)))
    def _():
        x_copy(i + 1).start()

    w1 = w1_ref[...].astype(jnp.bfloat16)
    z = jnp.dot(h_ref[...], w1, preferred_element_type=jnp.float32)
    z = jnp.square(jnp.maximum(z, 0.0)).astype(jnp.bfloat16)
    w2 = w2_ref[...].astype(jnp.bfloat16)
    o_ref[...] += jnp.dot(z, w2, preferred_element_type=jnp.float32)

    if apply_final_norm:
        @pl.when(f == pl.num_programs(1) - 1)
        def _():
            o_ref[...] = _rms_scale(o_ref[...]) * fg_ref[...]


def _mlp(x2d, g_all, w1_all, w2_all, layer, final_g, apply_final_norm):
    n, d = x2d.shape
    tm, tf = TM_MLP, TF_MLP
    assert n % tm == 0 and D_FF % tf == 0 and D_FF // tf >= 2
    return pl.pallas_call(
        functools.partial(_mlp_kernel, apply_final_norm=apply_final_norm),
        name="mlp",
        grid=(n // tm, D_FF // tf),
        in_specs=[
            pl.BlockSpec(memory_space=pl.ANY),
            _stacked_spec(g_all, layer),
            pl.BlockSpec((None, d, tf), lambda i, f: (layer, 0, f)),
            pl.BlockSpec((None, tf, d), lambda i, f: (layer, f, 0)),
            _const_spec((1, d)),
        ],
        out_specs=pl.BlockSpec((tm, d), lambda i, f: (i, 0)),
        out_shape=jax.ShapeDtypeStruct(x2d.shape, x2d.dtype),
        scratch_shapes=[
            pltpu.VMEM((tm, d), jnp.float32),
            pltpu.VMEM((tm, d), jnp.bfloat16),
            pltpu.SemaphoreType.DMA(()),
        ],
        compiler_params=pltpu.CompilerParams(
            dimension_semantics=("arbitrary", "arbitrary"),
            vmem_limit_bytes=VMEM_LIMIT_BYTES),
    )(x2d, g_all, w1_all, w2_all, final_g.reshape(1, -1))


def kernel(x, norm_mix_g, norm_mlp_g, w_in_even, conv_a_w, conv_a_b, ln_a_g, ln_a_b,
           conv_b_w, w_out_even, pool_w, pool_scale, mlp_w1, mlp_w2, final_g):
    bsz, seq, d = x.shape
    depth = norm_mix_g.shape[0]
    n_odd, n_groups, gdim, _ = pool_w.shape
    pool_w_rows = pool_w.reshape(n_odd, n_groups * gdim, gdim)
    rows = lambda stacked: stacked.reshape(stacked.shape[0], 1, stacked.shape[1])
    mix_g, mlp_g = rows(norm_mix_g), rows(norm_mlp_g)
    conv_a_b, ln_a_g, ln_a_b, pool_scale = rows(conv_a_b), rows(ln_a_g), rows(ln_a_b), rows(pool_scale)
    for layer in range(depth):
        i = layer // 2
        if layer % 2 == 0:
            x = _even_mixer(x, mix_g, layer, w_in_even, conv_a_w, conv_a_b, ln_a_g, ln_a_b,
                            conv_b_w, w_out_even, i)
        else:
            x = _pool_mixer(x, mix_g, layer, pool_w_rows, pool_scale, i)
        x = _mlp(x.reshape(bsz * seq, d), mlp_g, mlp_w1, mlp_w2, layer, final_g,
                 apply_final_norm=(layer == depth - 1)).reshape(bsz, seq, d)
    return x
```
